```python
import jax, jax.numpy as jnp
from jax import lax
import numpy as np

D_MODEL = 2048
BATCH = 2
SEQ = 16384
DEPTH = 4

N_MIXERS = 3
EPS = 1e-6
CHUNK = 128
SGU_WIDTH = D_MODEL
SGU_HEADS = 16
SGU_HEAD_DIM = SGU_WIDTH // SGU_HEADS
CONV_WIDTH = 31
POOL_WINDOWS = (2, 4, 8, 16)
POOL_GROUPS = len(POOL_WINDOWS)
POOL_GROUP_DIM = D_MODEL // POOL_GROUPS
D_FF = 4 * D_MODEL
N_SGU = (DEPTH + 2) // 3
N_CONV = (DEPTH + 1) // 3
N_POOL = DEPTH // 3

kernel_name = "hybrid_sgu_conv_pool_adaln_trunk"


def rmsnorm(x, g):
    xf = x.astype(jnp.float32)
    y = xf * lax.rsqrt(jnp.mean(xf * xf, axis=-1, keepdims=True) + EPS)
    return (y * g.astype(jnp.float32)).astype(x.dtype)


def layernorm(x, g, b):
    xf = x.astype(jnp.float32)
    mu = jnp.mean(xf, axis=-1, keepdims=True)
    var = jnp.mean(jnp.square(xf - mu), axis=-1, keepdims=True)
    y = (xf - mu) * lax.rsqrt(var + EPS) * g.astype(jnp.float32) + b.astype(jnp.float32)
    return y.astype(x.dtype)


def modulate(h, shift, scale):
    return h * (1 + scale[:, None, :]) + shift[:, None, :]


def chunked_sgu_mixer(h, w_in, b_in, v_g, v_b, w_s, b_s, w_out, b_out):
    B, S, _ = h.shape
    z = jax.nn.gelu(h @ w_in + b_in)
    u, v = jnp.split(z, 2, axis=-1)
    v = layernorm(v, v_g, v_b)
    v = v.reshape(B, S // CHUNK, CHUNK, SGU_HEADS, SGU_HEAD_DIM)
    causal = jnp.tril(jnp.ones((CHUNK, CHUNK), dtype=bool))
    w = jnp.where(causal, w_s, 0)
    sv = jnp.einsum('hpq,bnqhd->bnphd', w, v) + b_s.T[None, None, :, :, None]
    gated = u * sv.reshape(B, S, SGU_WIDTH)
    return gated @ w_out + b_out


def conformer_conv_mixer(h, w_pw1, b_pw1, w_dw, b_dw, ln_g, ln_b, w_pw2, b_pw2):
    a = jax.nn.glu(h @ w_pw1 + b_pw1, axis=-1)
    y = lax.conv_general_dilated(
        a, w_dw[:, None, :], window_strides=(1,),
        padding=((CONV_WIDTH - 1, 0),),
        dimension_numbers=('NWC', 'WIO', 'NWC'),
        feature_group_count=D_MODEL) + b_dw
    y = jax.nn.silu(layernorm(y, ln_g, ln_b))
    return y @ w_pw2 + b_pw2


def multiscale_pool_mixer(h, w_pool, b_pool, layer_scale):
    B, S, D = h.shape
    hf = h.astype(jnp.float32)
    cs = jnp.cumsum(hf, axis=1)
    count_base = jnp.arange(1, S + 1, dtype=jnp.float32)[None, :, None]
    outs = []
    for g, win in enumerate(POOL_WINDOWS):
        sl = slice(g * POOL_GROUP_DIM, (g + 1) * POOL_GROUP_DIM)
        c_g = cs[..., sl]
        prev = jnp.pad(c_g, ((0, 0), (win, 0), (0, 0)))[:, :S]
        mean = (c_g - prev) / jnp.minimum(count_base, float(win))
        outs.append(mean - hf[..., sl])
    p = jnp.stack(outs, axis=2).astype(h.dtype)
    y = jnp.einsum('bsgi,gio->bsgo', p, w_pool) + b_pool
    return y.reshape(B, S, D) * layer_scale


def squared_relu_mlp(h, w1, w2):
    return jnp.square(jax.nn.relu(h @ w1)) @ w2


def setup_inputs(seed: int = 0) -> dict:
    key = jax.random.key(seed)
    ks = iter(jax.random.split(key, 32))

    def nrm(shape, scale):
        return jax.random.normal(next(ks), shape, dtype=jnp.float32) * scale

    def gain(shape):
        return 1.0 + nrm(shape, 0.02)

    D = D_MODEL
    return {
        "x": nrm((BATCH, SEQ, D), 1.0),
        "c": nrm((BATCH, D), 1.0),
        "w_ada": nrm((DEPTH, D, 6 * D), 0.5 * D ** -0.5),
        "b_ada": nrm((DEPTH, 6 * D), 0.01),
        "norm_mix_g": gain((DEPTH, D)),
        "norm_ffn_g": gain((DEPTH, D)),
        "sgu_w_in": nrm((N_SGU, D, 2 * SGU_WIDTH), D ** -0.5),
        "sgu_b_in": nrm((N_SGU, 2 * SGU_WIDTH), 0.01),
        "sgu_v_g": gain((N_SGU, SGU_WIDTH)),
        "sgu_v_b": nrm((N_SGU, SGU_WIDTH), 0.01),
        "sgu_w_s": nrm((N_SGU, SGU_HEADS, CHUNK, CHUNK), CHUNK ** -0.5),
        "sgu_b_s": gain((N_SGU, SGU_HEADS, CHUNK)),
        "sgu_w_out": nrm((N_SGU, SGU_WIDTH, D), SGU_WIDTH ** -0.5),
        "sgu_b_out": nrm((N_SGU, D), 0.01),
        "conv_w_pw1": nrm((N_CONV, D, 2 * D), D ** -0.5),
        "conv_b_pw1": nrm((N_CONV, 2 * D), 0.01),
        "conv_w_dw": nrm((N_CONV, CONV_WIDTH, D), CONV_WIDTH ** -0.5),
        "conv_b_dw": nrm((N_CONV, D), 0.01),
        "conv_ln_g": gain((N_CONV, D)),
        "conv_ln_b": nrm((N_CONV, D), 0.01),
        "conv_w_pw2": nrm((N_CONV, D, D), D ** -0.5),
        "conv_b_pw2": nrm((N_CONV, D), 0.01),
        "pool_w": nrm((N_POOL, POOL_GROUPS, POOL_GROUP_DIM, POOL_GROUP_DIM), POOL_GROUP_DIM ** -0.5),
        "pool_b": nrm((N_POOL, POOL_GROUPS, POOL_GROUP_DIM), 0.01),
        "pool_scale": gain((N_POOL, D)),
        "mlp_w1": nrm((DEPTH, D, D_FF), D ** -0.5),
        "mlp_w2": nrm((DEPTH, D_FF, D), D_FF ** -0.5),
        "final_g": gain((D,)),
    }


def reference(x, c, w_ada, b_ada, norm_mix_g, norm_ffn_g,
              sgu_w_in, sgu_b_in, sgu_v_g, sgu_v_b, sgu_w_s, sgu_b_s, sgu_w_out, sgu_b_out,
              conv_w_pw1, conv_b_pw1, conv_w_dw, conv_b_dw, conv_ln_g, conv_ln_b, conv_w_pw2, conv_b_pw2,
              pool_w, pool_b, pool_scale, mlp_w1, mlp_w2, final_g):
    c_act = jax.nn.silu(c)
    for i in range(DEPTH):
        mod = c_act @ w_ada[i] + b_ada[i]
        shift_m, scale_m, gate_m, shift_f, scale_f, gate_f = jnp.split(mod, 6, axis=-1)

        h = modulate(rmsnorm(x, norm_mix_g[i]), shift_m, scale_m)
        kind, j = i % N_MIXERS, i // N_MIXERS
        if kind == 0:
            y = chunked_sgu_mixer(h, sgu_w_in[j], sgu_b_in[j], sgu_v_g[j], sgu_v_b[j],
                                  sgu_w_s[j], sgu_b_s[j], sgu_w_out[j], sgu_b_out[j])
        elif kind == 1:
            y = conformer_conv_mixer(h, conv_w_pw1[j], conv_b_pw1[j], conv_w_dw[j], conv_b_dw[j],
                                     conv_ln_g[j], conv_ln_b[j], conv_w_pw2[j], conv_b_pw2[j])
        else:
            y = multiscale_pool_mixer(h, pool_w[j], pool_b[j], pool_scale[j])
        x = x + gate_m[:, None, :] * y

        h = modulate(rmsnorm(x, norm_ffn_g[i]), shift_f, scale_f)
        x = x + gate_f[:, None, :] * squared_relu_mlp(h, mlp_w1[i], mlp_w2[i])
    return rmsnorm(x, final_g)
```

```python
import functools

import jax
import jax.numpy as jnp
from jax import lax
from jax.experimental import pallas as pl
from jax.experimental.pallas import tpu as pltpu

EPS = 1e-6
N_MIXERS = 3
SGU_CHUNK = 128
CONV_WIDTH = 31
CONV_HALO = 32
POOL_WINDOWS = (2, 4, 8, 16)
POOL_HALO = 16
SUBLANES, LANES = 8, 128
MOD_ROWS = SUBLANES

V7X_VMEM_BYTES = 64 * 1024 * 1024
VMEM_LIMIT_BYTES = V7X_VMEM_BYTES - 8 * 1024 * 1024

F32 = jnp.float32
BF16 = jnp.bfloat16


def _compiler_params(n_axes):
    return pltpu.CompilerParams(
        dimension_semantics=("arbitrary",) * n_axes,
        vmem_limit_bytes=VMEM_LIMIT_BYTES,
    )


def _rms_modulate(x, g, shift, scale):
    ms = jnp.mean(x * x, axis=-1, keepdims=True)
    y = x * lax.rsqrt(ms + EPS)
    return (y * g) * (1.0 + scale) + shift


def _layernorm(v, g, b):
    mu = jnp.mean(v, axis=-1, keepdims=True)
    d = v - mu
    var = jnp.mean(d * d, axis=-1, keepdims=True)
    return d * lax.rsqrt(var + EPS) * g + b


def _dot(a, b):
    return jnp.dot(a, b, preferred_element_type=F32)


def _mod_kernel(c_ref, w_ref, b_ref, o_ref):
    c = c_ref[...]
    c_act = (c * jax.nn.sigmoid(c)).astype(BF16)
    o_ref[...] = _dot(c_act, w_ref[...].astype(BF16)) + b_ref[...]


def _modulation(c, w_ada, b_ada, tn=1024):
    depth, d, n = w_ada.shape
    c_pad = jnp.pad(c, ((0, MOD_ROWS - c.shape[0]), (0, 0)))
    return pl.pallas_call(
        _mod_kernel,
        grid=(depth, n // tn),
        in_specs=[
            pl.BlockSpec((MOD_ROWS, d), lambda l, j: (0, 0)),
            pl.BlockSpec((None, d, tn), lambda l, j: (l, 0, j)),
            pl.BlockSpec((None, 1, tn), lambda l, j: (l, 0, j)),
        ],
        out_specs=pl.BlockSpec((None, MOD_ROWS, tn), lambda l, j: (l, 0, j)),
        out_shape=jax.ShapeDtypeStruct((depth, MOD_ROWS, n), F32),
        compiler_params=_compiler_params(2),
        name="adaln_modulation",
    )(c_pad, w_ada, b_ada.reshape(depth, 1, n))


def _mod_specs(layer, first, n_grid_axes, d):
    def spec(k):
        if n_grid_axes == 1:
            return pl.BlockSpec((None, MOD_ROWS, d), lambda i: (layer, 0, k))
        return pl.BlockSpec((None, MOD_ROWS, d), lambda i, s: (layer, 0, k))
    return [spec(first), spec(first + 1), spec(first + 2)]


def _mlp_kernel(x_ref, sh_ref, sc_ref, gt_ref, g_ref, w1_ref, w2_ref, *rest,
                tiles_per_batch, n_ff_steps, final_norm):
    if final_norm:
        fg_ref, o_ref, h_ref = rest
    else:
        o_ref, h_ref = rest
    f = pl.program_id(1)
    b = pl.program_id(0) // tiles_per_batch

    @pl.when(f == 0)
    def _():
        x = x_ref[...]
        h = _rms_modulate(x, g_ref[...], sh_ref[pl.ds(b, 1), :], sc_ref[pl.ds(b, 1), :])
        h_ref[...] = h.astype(BF16)
        o_ref[...] = x

    u = jnp.maximum(_dot(h_ref[...], w1_ref[...]), 0.0)
    u = (u * u).astype(BF16)
    o_ref[...] += gt_ref[pl.ds(b, 1), :] * _dot(u, w2_ref[...])

    if final_norm:
        @pl.when(f == n_ff_steps - 1)
        def _():
            o = o_ref[...]
            ms = jnp.mean(o * o, axis=-1, keepdims=True)
            o_ref[...] = o * lax.rsqrt(ms + EPS) * fg_ref[...]


def _mlp_sublayer(x, mods, layer, g, w1, w2, final_g, *, seq, tm, tf):
    t, d = x.shape
    ff = w1.shape[1]
    n_ff_steps = ff // tf
    final_norm = final_g is not None
    in_specs = [
        pl.BlockSpec((tm, d), lambda i, f: (i, 0)),
        *_mod_specs(layer, 3, 2, d),
        pl.BlockSpec((1, d), lambda i, f: (0, 0)),
        pl.BlockSpec((d, tf), lambda i, f: (0, f)),
        pl.BlockSpec((tf, d), lambda i, f: (f, 0)),
    ]
    args = [x, mods, mods, mods, g, w1, w2]
    if final_norm:
        in_specs.append(pl.BlockSpec((1, d), lambda i, f: (0, 0)))
        args.append(final_g)
    return pl.pallas_call(
        functools.partial(_mlp_kernel, tiles_per_batch=seq // tm, n_ff_steps=n_ff_steps,
                          final_norm=final_norm),
        grid=(t // tm, n_ff_steps),
        in_specs=in_specs,
        out_specs=pl.BlockSpec((tm, d), lambda i, f: (i, 0)),
        out_shape=jax.ShapeDtypeStruct((t, d), F32),
        scratch_shapes=[pltpu.VMEM((tm, d), BF16)],
        compiler_params=_compiler_params(2),
        name="mlp_sublayer",
    )(*args)


def _sgu_kernel(x_ref, sh_ref, sc_ref, gt_ref, g_ref, win_ref, bin_ref, vg_ref, vb_ref,
                ws_ref, bs_ref, wout_ref, bout_ref, o_ref, h_ref, v_ref, mu_ref, rs_ref, sv_ref,
                *, tiles_per_batch, n_groups, cw):
    s = pl.program_id(1)
    b = pl.program_id(0) // tiles_per_batch
    tm = x_ref.shape[0]
    gate = gt_ref[pl.ds(b, 1), :]

    @pl.when(s == 0)
    def _():
        x = x_ref[...]
        h = _rms_modulate(x, g_ref[...], sh_ref[pl.ds(b, 1), :], sc_ref[pl.ds(b, 1), :])
        h_ref[...] = h.astype(BF16)
        o_ref[...] = x + gate * bout_ref[...]

    z = jax.nn.gelu(_dot(h_ref[...], win_ref[...]) + bin_ref[...])

    for j in range(n_groups):
        @pl.when(s == j)
        def _(j=j):
            v_ref[:, j * cw:(j + 1) * cw] = z

    @pl.when(s == n_groups - 1)
    def _():
        v = v_ref[...]
        mu = jnp.mean(v, axis=-1, keepdims=True)
        dv = v - mu
        mu_ref[...] = mu
        rs_ref[...] = lax.rsqrt(jnp.mean(dv * dv, axis=-1, keepdims=True) + EPS)

    rows = lax.broadcasted_iota(jnp.int32, (SGU_CHUNK, SGU_CHUNK), 0)
    cols = lax.broadcasted_iota(jnp.int32, (SGU_CHUNK, SGU_CHUNK), 1)
    causal = rows >= cols

    for j in range(n_groups):
        @pl.when(s == n_groups + j)
        def _(j=j):
            cs = slice(j * cw, (j + 1) * cw)
            vn = (v_ref[:, cs] - mu_ref[...]) * rs_ref[...] * vg_ref[:, cs] + vb_ref[:, cs]
            vn = vn.astype(BF16)
            for hh in range(cw // SGU_CHUNK):
                hs = slice(hh * SGU_CHUNK, (hh + 1) * SGU_CHUNK)
                w = jnp.where(causal, ws_ref[hh], 0.0).astype(BF16)
                bias = bs_ref[hh]
                for n in range(tm // SGU_CHUNK):
                    ns = slice(n * SGU_CHUNK, (n + 1) * SGU_CHUNK)
                    sv_ref[ns, hs] = _dot(w, vn[ns, hs]) + bias
            gated = (z * sv_ref[...]).astype(BF16)
            o_ref[...] += gate * _dot(gated, wout_ref[...])


def _sgu_sublayer(x, mods, layer, g, w_in, b_in, v_g, v_b, w_s, b_s, w_out, b_out, *, seq, tm, cw):
    t, d = x.shape
    width = w_out.shape[0]
    heads = w_s.shape[0]
    n_groups = width // cw
    hpg = cw // SGU_CHUNK
    u_group = lambda s: jnp.maximum(s - n_groups, 0)
    return pl.pallas_call(
        functools.partial(_sgu_kernel, tiles_per_batch=seq // tm, n_groups=n_groups, cw=cw),
        grid=(t // tm, 2 * n_groups),
        in_specs=[
            pl.BlockSpec((tm, d), lambda i, s: (i, 0)),
            *_mod_specs(layer, 0, 2, d),
            pl.BlockSpec((1, d), lambda i, s: (0, 0)),
            pl.BlockSpec((d, cw), lambda i, s: (0, (s + n_groups) % (2 * n_groups))),
            pl.BlockSpec((1, cw), lambda i, s: (0, (s + n_groups) % (2 * n_groups))),
            pl.BlockSpec((1, width), lambda i, s: (0, 0)),
            pl.BlockSpec((1, width), lambda i, s: (0, 0)),
            pl.BlockSpec((hpg, SGU_CHUNK, SGU_CHUNK), lambda i, s: (u_group(s), 0, 0)),
            pl.BlockSpec((hpg, SGU_CHUNK, 1), lambda i, s: (u_group(s), 0, 0)),
            pl.BlockSpec((cw, d), lambda i, s: (u_group(s), 0)),
            pl.BlockSpec((1, d), lambda i, s: (0, 0)),
        ],
        out_specs=pl.BlockSpec((tm, d), lambda i, s: (i, 0)),
        out_shape=jax.ShapeDtypeStruct((t, d), F32),
        scratch_shapes=[
            pltpu.VMEM((tm, d), BF16),
            pltpu.VMEM((tm, width), F32),
            pltpu.VMEM((tm, 1), F32),
            pltpu.VMEM((tm, 1), F32),
            pltpu.VMEM((tm, cw), F32),
        ],
        compiler_params=_compiler_params(2),
        name="sgu_sublayer",
    )(x, mods, mods, mods, g, w_in, b_in, v_g, v_b, w_s, b_s.reshape(heads, SGU_CHUNK, 1), w_out, b_out)


def _conv_kernel(x_ref, sh_ref, sc_ref, gt_ref, g_ref, w1v_ref, w1g_ref, b1v_ref, b1g_ref,
                 wdw_ref, bdw_ref, lng_ref, lnb_ref, w2_ref, b2_ref, o_ref, h_ref, a_ref, y_ref, yn_ref,
                 *, tiles_per_batch, n_groups, cw, norm_rows_per_step):
    i = pl.program_id(0)
    s = pl.program_id(1)
    b = i // tiles_per_batch
    tm, d = x_ref.shape

    @pl.when(s == 0)
    def _():
        x = x_ref[...]
        h = _rms_modulate(x, g_ref[...], sh_ref[pl.ds(b, 1), :], sc_ref[pl.ds(b, 1), :])
        h_ref[...] = h.astype(BF16)

    @pl.when(s < n_groups)
    def _():
        h = h_ref[...]
        val = _dot(h, w1v_ref[...]) + b1v_ref[...]
        gat = _dot(h, w1g_ref[...]) + b1g_ref[...]
        a = val * jax.nn.sigmoid(gat)
        for j in range(n_groups):
            @pl.when(s == j)
            def _(j=j):
                a_ref[CONV_HALO:, j * cw:(j + 1) * cw] = a

    @pl.when(s == n_groups - 1)
    def _():
        @pl.when(i % tiles_per_batch == 0)
        def _():
            a_ref[0:CONV_HALO, :] = jnp.zeros((CONV_HALO, d), F32)

        first_tap = CONV_HALO - (CONV_WIDTH - 1)

        def conv_column(c, carry):
            cs = pl.ds(pl.multiple_of(c * LANES, LANES), LANES)
            w = wdw_ref[:, cs]
            taps = [jnp.broadcast_to(w[k:k + 1, :], (SUBLANES, LANES)) for k in range(CONV_WIDTH)]
            bias = jnp.broadcast_to(bdw_ref[:, cs], (SUBLANES, LANES))
            for m in range(tm // SUBLANES):
                acc = bias
                for k in range(CONV_WIDTH):
                    r = m * SUBLANES + first_tap + k
                    acc = acc + taps[k] * a_ref[r:r + SUBLANES, cs]
                y_ref[m * SUBLANES:(m + 1) * SUBLANES, cs] = acc
            return carry

        lax.fori_loop(0, d // LANES, conv_column, 0)
        a_ref[0:CONV_HALO, :] = a_ref[tm:tm + CONV_HALO, :]

        def norm_rows(r, carry):
            rs = pl.ds(pl.multiple_of(r * norm_rows_per_step, norm_rows_per_step), norm_rows_per_step)
            y = _layernorm(y_ref[rs, :], lng_ref[...], lnb_ref[...])
            yn_ref[rs, :] = (y * jax.nn.sigmoid(y)).astype(BF16)
            return carry

        lax.fori_loop(0, tm // norm_rows_per_step, norm_rows, 0)

    for j in range(n_groups):
        @pl.when(s == n_groups + j)
        def _(j=j):
            cs = slice(j * cw, (j + 1) * cw)
            y2 = _dot(yn_ref[...], w2_ref[...]) + b2_ref[:, cs]
            o_ref[:, cs] = x_ref[:, cs] + gt_ref[pl.ds(b, 1), cs] * y2


def _conv_sublayer(x, mods, layer, g, w_pw1, b_pw1, w_dw, b_dw, ln_g, ln_b, w_pw2, b_pw2,
                   *, seq, tm, cw, norm_rows_per_step=32):
    t, d = x.shape
    n_groups = d // cw
    glu_group = lambda s: jnp.minimum(s, n_groups - 1)
    out_group = lambda s: jnp.maximum(s - n_groups, 0)
    return pl.pallas_call(
        functools.partial(_conv_kernel, tiles_per_batch=seq // tm, n_groups=n_groups, cw=cw,
                          norm_rows_per_step=norm_rows_per_step),
        grid=(t // tm, 2 * n_groups),
        in_specs=[
            pl.BlockSpec((tm, d), lambda i, s: (i, 0)),
            *_mod_specs(layer, 0, 2, d),
            pl.BlockSpec((1, d), lambda i, s: (0, 0)),
            pl.BlockSpec((d, cw), lambda i, s: (0, glu_group(s))),
            pl.BlockSpec((d, cw), lambda i, s: (0, n_groups + glu_group(s))),
            pl.BlockSpec((1, cw), lambda i, s: (0, glu_group(s))),
            pl.BlockSpec((1, cw), lambda i, s: (0, n_groups + glu_group(s))),
            pl.BlockSpec((CONV_WIDTH, d), lambda i, s: (0, 0)),
            pl.BlockSpec((1, d), lambda i, s: (0, 0)),
            pl.BlockSpec((1, d), lambda i, s: (0, 0)),
            pl.BlockSpec((1, d), lambda i, s: (0, 0)),
            pl.BlockSpec((d, cw), lambda i, s: (0, out_group(s))),
            pl.BlockSpec((1, d), lambda i, s: (0, 0)),
        ],
        out_specs=pl.BlockSpec((tm, d), lambda i, s: (i, 0)),
        out_shape=jax.ShapeDtypeStruct((t, d), F32),
        scratch_shapes=[
            pltpu.VMEM((tm, d), BF16),
            pltpu.VMEM((tm + CONV_HALO, d), F32),
            pltpu.VMEM((tm, d), F32),
            pltpu.VMEM((tm, d), BF16),
        ],
        compiler_params=_compiler_params(2),
        name="conv_sublayer",
    )(x, mods, mods, mods, g, w_pw1, w_pw1, b_pw1, b_pw1, w_dw, b_dw, ln_g, ln_b, w_pw2, b_pw2)


def _pool_kernel(x_ref, sh_ref, sc_ref, gt_ref, g_ref, pw_ref, pb_ref, ls_ref, o_ref, hx_ref,
                 *, tiles_per_batch):
    i = pl.program_id(0)
    b = i // tiles_per_batch
    tile_in_seq = i % tiles_per_batch
    tm, d = x_ref.shape
    gd = d // len(POOL_WINDOWS)

    @pl.when(tile_in_seq == 0)
    def _():
        hx_ref[0:POOL_HALO, :] = jnp.zeros((POOL_HALO, d), F32)

    x = x_ref[...]
    hx_ref[POOL_HALO:, :] = _rms_modulate(x, g_ref[...], sh_ref[pl.ds(b, 1), :], sc_ref[pl.ds(b, 1), :])
    pos = tile_in_seq * tm + lax.broadcasted_iota(jnp.int32, (tm, 1), 0)
    gate = gt_ref[pl.ds(b, 1), :]

    for gi, win in enumerate(POOL_WINDOWS):
        cs = slice(gi * gd, (gi + 1) * gd)
        h = hx_ref[POOL_HALO:, cs]
        tot = h
        for back in range(1, win):
            tot = tot + hx_ref[POOL_HALO - back:POOL_HALO - back + tm, cs]
        count = jnp.minimum(pos + 1, win).astype(F32)
        p = (tot / count - h).astype(BF16)
        y = (_dot(p, pw_ref[gi]) + pb_ref[:, cs]) * ls_ref[:, cs]
        o_ref[:, cs] = x[:, cs] + gate[:, cs] * y

    hx_ref[0:POOL_HALO, :] = hx_ref[tm:tm + POOL_HALO, :]


def _pool_sublayer(x, mods, layer, g, pool_w, pool_b, pool_scale, *, seq, tm):
    t, d = x.shape
    groups, gd, _ = pool_w.shape
    return pl.pallas_call(
        functools.partial(_pool_kernel, tiles_per_batch=seq // tm),
        grid=(t // tm,),
        in_specs=[
            pl.BlockSpec((tm, d), lambda i: (i, 0)),
            *_mod_specs(layer, 0, 1, d),
            pl.BlockSpec((1, d), lambda i: (0, 0)),
            pl.BlockSpec((groups, gd, gd), lambda i: (0, 0, 0)),
            pl.BlockSpec((1, d), lambda i: (0, 0)),
            pl.BlockSpec((1, d), lambda i: (0, 0)),
        ],
        out_specs=pl.BlockSpec((tm, d), lambda i: (i, 0)),
        out_shape=jax.ShapeDtypeStruct((t, d), F32),
        scratch_shapes=[pltpu.VMEM((tm + POOL_HALO, d), F32)],
        compiler_params=_compiler_params(1),
        name="pool_sublayer",
    )(x, mods, mods, mods, g, pool_w, pool_b, pool_scale)


def kernel(x, c, w_ada, b_ada, norm_mix_g, norm_ffn_g,
           sgu_w_in, sgu_b_in, sgu_v_g, sgu_v_b, sgu_w_s, sgu_b_s, sgu_w_out, sgu_b_out,
           conv_w_pw1, conv_b_pw1, conv_w_dw, conv_b_dw, conv_ln_g, conv_ln_b, conv_w_pw2, conv_b_pw2,
           pool_w, pool_b, pool_scale, mlp_w1, mlp_w2, final_g):
    batch, seq, d = x.shape
    depth = w_ada.shape[0]
    row = lambda v: v.reshape(1, -1)

    mods = _modulation(c, w_ada, b_ada)
    xt = x.reshape(batch * seq, d)

    for layer in range(depth):
        kind, j = layer % N_MIXERS, layer // N_MIXERS
        g_mix = row(norm_mix_g[layer])
        if kind == 0:
            xt = _sgu_sublayer(
                xt, mods, layer, g_mix, sgu_w_in[j].astype(BF16), row(sgu_b_in[j]),
                row(sgu_v_g[j]), row(sgu_v_b[j]), sgu_w_s[j], sgu_b_s[j],
                sgu_w_out[j].astype(BF16), row(sgu_b_out[j]), seq=seq, tm=512, cw=512)
        elif kind == 1:
            xt = _conv_sublayer(
                xt, mods, layer, g_mix, conv_w_pw1[j].astype(BF16), row(conv_b_pw1[j]),
                conv_w_dw[j], row(conv_b_dw[j]), row(conv_ln_g[j]), row(conv_ln_b[j]),
                conv_w_pw2[j].astype(BF16), row(conv_b_pw2[j]), seq=seq, tm=512, cw=512)
        else:
            xt = _pool_sublayer(
                xt, mods, layer, g_mix, pool_w[j].astype(BF16), row(pool_b[j]), row(pool_scale[j]),
                seq=seq, tm=512)
        xt = _mlp_sublayer(
            xt, mods, layer, row(norm_ffn_g[layer]), mlp_w1[layer].astype(BF16),
            mlp_w2[layer].astype(BF16), row(final_g) if layer == depth - 1 else None,
            seq=seq, tm=512, tf=1024)

    return xt.reshape(batch, seq, d)
```

```python
import functools

import jax
import jax.numpy as jnp
from jax import lax
from jax.experimental import pallas as pl
from jax.experimental.pallas import tpu as pltpu

EPS = 1e-6
N_MIXERS = 3
SGU_CHUNK = 128
CONV_WIDTH = 31
CONV_HALO = 32
POOL_WINDOWS = (2, 4, 8, 16)
POOL_HALO = 16
SUBLANES, LANES = 8, 128
MOD_ROWS = SUBLANES

V7X_VMEM_BYTES = 64 * 1024 * 1024
VMEM_LIMIT_BYTES = V7X_VMEM_BYTES - 4 * 1024 * 1024

F32 = jnp.float32
BF16 = jnp.bfloat16


def _compiler_params(n_axes):
    return pltpu.CompilerParams(
        dimension_semantics=("arbitrary",) * n_axes,
        vmem_limit_bytes=VMEM_LIMIT_BYTES,
    )


def _rms_modulate(x, g, shift, scale):
    ms = jnp.mean(x * x, axis=-1, keepdims=True)
    y = x * lax.rsqrt(ms + EPS)
    return (y * g) * (1.0 + scale) + shift


def _layernorm(v, g, b):
    mu = jnp.mean(v, axis=-1, keepdims=True)
    d = v - mu
    var = jnp.mean(d * d, axis=-1, keepdims=True)
    return d * lax.rsqrt(var + EPS) * g + b


def _dot(a, b):
    return jnp.dot(a, b, preferred_element_type=F32)


def _mod_kernel(c_ref, w_ref, b_ref, o_ref):
    c = c_ref[...]
    c_act = (c * jax.nn.sigmoid(c)).astype(BF16)
    o_ref[...] = _dot(c_act, w_ref[...].astype(BF16)) + b_ref[...]


def _modulation(c, w_ada, b_ada, tn=1024):
    depth, d, n = w_ada.shape
    c_pad = jnp.pad(c, ((0, MOD_ROWS - c.shape[0]), (0, 0)))
    return pl.pallas_call(
        _mod_kernel,
        grid=(depth, n // tn),
        in_specs=[
            pl.BlockSpec((MOD_ROWS, d), lambda l, j: (0, 0)),
            pl.BlockSpec((None, d, tn), lambda l, j: (l, 0, j)),
            pl.BlockSpec((None, 1, tn), lambda l, j: (l, 0, j)),
        ],
        out_specs=pl.BlockSpec((None, MOD_ROWS, tn), lambda l, j: (l, 0, j)),
        out_shape=jax.ShapeDtypeStruct((depth, MOD_ROWS, n), F32),
        compiler_params=_compiler_params(2),
        name="adaln_modulation",
    )(c_pad, w_ada, b_ada.reshape(depth, 1, n))


def _mod_specs(layer, first, n_grid_axes, d):
    def spec(k):
        if n_grid_axes == 1:
            return pl.BlockSpec((None, MOD_ROWS, d), lambda i: (layer, 0, k))
        return pl.BlockSpec((None, MOD_ROWS, d), lambda i, s: (layer, 0, k))
    return [spec(first), spec(first + 1), spec(first + 2)]


def _mlp_kernel(x_ref, sh_ref, sc_ref, gt_ref, g_ref, w1_ref, w2_ref, *rest,
                tiles_per_batch, n_ff_steps, final_norm):
    if final_norm:
        fg_ref, o_ref, h_ref = rest
    else:
        o_ref, h_ref = rest
    f = pl.program_id(1)
    b = pl.program_id(0) // tiles_per_batch

    @pl.when(f == 0)
    def _():
        x = x_ref[...]
        h = _rms_modulate(x, g_ref[...], sh_ref[pl.ds(b, 1), :], sc_ref[pl.ds(b, 1), :])
        h_ref[...] = h.astype(BF16)
        o_ref[...] = x

    u = jnp.maximum(_dot(h_ref[...], w1_ref[...]), 0.0)
    u = (u * u).astype(BF16)
    o_ref[...] += gt_ref[pl.ds(b, 1), :] * _dot(u, w2_ref[...])

    if final_norm:
        @pl.when(f == n_ff_steps - 1)
        def _():
            o = o_ref[...]
            ms = jnp.mean(o * o, axis=-1, keepdims=True)
            o_ref[...] = o * lax.rsqrt(ms + EPS) * fg_ref[...]


def _mlp_sublayer(x, mods, layer, g, w1, w2, final_g, *, seq, tm, tf):
    t, d = x.shape
    ff = w1.shape[1]
    n_ff_steps = ff // tf
    final_norm = final_g is not None
    in_specs = [
        pl.BlockSpec((tm, d), lambda i, f: (i, 0)),
        *_mod_specs(layer, 3, 2, d),
        pl.BlockSpec((1, d), lambda i, f: (0, 0)),
        pl.BlockSpec((d, tf), lambda i, f: (0, f)),
        pl.BlockSpec((tf, d), lambda i, f: (f, 0)),
    ]
    args = [x, mods, mods, mods, g, w1, w2]
    if final_norm:
        in_specs.append(pl.BlockSpec((1, d), lambda i, f: (0, 0)))
        args.append(final_g)
    return pl.pallas_call(
        functools.partial(_mlp_kernel, tiles_per_batch=seq // tm, n_ff_steps=n_ff_steps,
                          final_norm=final_norm),
        grid=(t // tm, n_ff_steps),
        in_specs=in_specs,
        out_specs=pl.BlockSpec((tm, d), lambda i, f: (i, 0)),
        out_shape=jax.ShapeDtypeStruct((t, d), F32),
        scratch_shapes=[pltpu.VMEM((tm, d), BF16)],
        compiler_params=_compiler_params(2),
        name="mlp_sublayer",
    )(*args)


def _resident(shape):
    return pl.BlockSpec(shape, lambda i: (0,) * len(shape), pipeline_mode=pl.Buffered(1))


def _sgu_kernel(x_ref, sh_ref, sc_ref, gt_ref, g_ref, win_ref, bin_ref, vg_ref, vb_ref,
                ws_ref, bs_ref, wout_ref, bout_ref, o_ref, *, tiles_per_batch, cw):
    b = pl.program_id(0) // tiles_per_batch
    tm = x_ref.shape[0]
    width = wout_ref.shape[0]
    n_groups = width // cw

    x = x_ref[...]
    h = _rms_modulate(x, g_ref[...], sh_ref[pl.ds(b, 1), :], sc_ref[pl.ds(b, 1), :]).astype(BF16)

    def in_proj(col0):
        cs = slice(col0, col0 + cw)
        return jax.nn.gelu(_dot(h, win_ref[:, cs]) + bin_ref[:, cs])

    v = jnp.concatenate([in_proj(width + j * cw) for j in range(n_groups)], axis=-1)
    vn = _layernorm(v, vg_ref[...], vb_ref[...]).astype(BF16)

    rows = lax.broadcasted_iota(jnp.int32, (SGU_CHUNK, SGU_CHUNK), 0)
    cols = lax.broadcasted_iota(jnp.int32, (SGU_CHUNK, SGU_CHUNK), 1)
    causal = rows >= cols

    gated = []
    for j in range(n_groups):
        u = in_proj(j * cw)
        sv_heads = []
        for hh in range(j * cw // SGU_CHUNK, (j + 1) * cw // SGU_CHUNK):
            hs = slice(hh * SGU_CHUNK, (hh + 1) * SGU_CHUNK)
            w = jnp.where(causal, ws_ref[hh], 0.0).astype(BF16)
            bias = bs_ref[hh]
            sv_heads.append(jnp.concatenate(
                [_dot(w, vn[n * SGU_CHUNK:(n + 1) * SGU_CHUNK, hs]) + bias for n in range(tm // SGU_CHUNK)],
                axis=0))
        gated.append((u * jnp.concatenate(sv_heads, axis=-1)).astype(BF16))

    y = _dot(jnp.concatenate(gated, axis=-1), wout_ref[...]) + bout_ref[...]
    o_ref[...] = x + gt_ref[pl.ds(b, 1), :] * y


def _sgu_sublayer(x, mods, layer, g, w_in, b_in, v_g, v_b, w_s, b_s, w_out, b_out, *, seq, tm, cw):
    t, d = x.shape
    width = w_out.shape[0]
    heads = w_s.shape[0]
    return pl.pallas_call(
        functools.partial(_sgu_kernel, tiles_per_batch=seq // tm, cw=cw),
        grid=(t // tm,),
        in_specs=[
            pl.BlockSpec((tm, d), lambda i: (i, 0)),
            *_mod_specs(layer, 0, 1, d),
            _resident((1, d)),
            _resident((d, 2 * width)),
            _resident((1, 2 * width)),
            _resident((1, width)),
            _resident((1, width)),
            _resident((heads, SGU_CHUNK, SGU_CHUNK)),
            _resident((heads, SGU_CHUNK, 1)),
            _resident((width, d)),
            _resident((1, d)),
        ],
        out_specs=pl.BlockSpec((tm, d), lambda i: (i, 0)),
        out_shape=jax.ShapeDtypeStruct((t, d), F32),
        compiler_params=_compiler_params(1),
        name="sgu_sublayer",
    )(x, mods, mods, mods, g, w_in, b_in, v_g, v_b, w_s, b_s.reshape(heads, SGU_CHUNK, 1), w_out, b_out)


def _conv_kernel(x_ref, sh_ref, sc_ref, gt_ref, g_ref, w1_ref, b1_ref, wdw_ref, bdw_ref, lng_ref, lnb_ref,
                 w2_ref, b2_ref, o_ref, a_ref, *, tiles_per_batch, cw, conv_rows):
    i = pl.program_id(0)
    b = i // tiles_per_batch
    tm, d = x_ref.shape
    first_tap = CONV_HALO - (CONV_WIDTH - 1)

    @pl.when(i % tiles_per_batch == 0)
    def _():
        a_ref[0:CONV_HALO, :] = jnp.zeros((CONV_HALO, d), F32)

    x = x_ref[...]
    h = _rms_modulate(x, g_ref[...], sh_ref[pl.ds(b, 1), :], sc_ref[pl.ds(b, 1), :]).astype(BF16)

    for j in range(d // cw):
        cs = slice(j * cw, (j + 1) * cw)
        gs = slice(d + j * cw, d + (j + 1) * cw)
        val = _dot(h, w1_ref[:, cs]) + b1_ref[:, cs]
        gat = _dot(h, w1_ref[:, gs]) + b1_ref[:, gs]
        a_ref[CONV_HALO:, cs] = val * jax.nn.sigmoid(gat)

        for c in range(j * cw // LANES, (j + 1) * cw // LANES):
            ls = slice(c * LANES, (c + 1) * LANES)
            taps = [wdw_ref[k:k + 1, ls] for k in range(CONV_WIDTH)]
            bias = bdw_ref[:, ls]
            for m in range(tm // conv_rows):
                base = m * conv_rows + first_tap
                total = None
                for r in range(SUBLANES):
                    part = None
                    for k in range(r, CONV_WIDTH, SUBLANES):
                        term = taps[k] * a_ref[base + k:base + k + conv_rows, ls]
                        part = term if part is None else part + term
                    total = part if total is None else total + part
                a_ref[m * conv_rows:(m + 1) * conv_rows, ls] = total + bias

    y = _layernorm(a_ref[0:tm, :], lng_ref[...], lnb_ref[...])
    a_ref[0:CONV_HALO, :] = a_ref[tm:tm + CONV_HALO, :]
    yn = (y * jax.nn.sigmoid(y)).astype(BF16)
    y2 = _dot(yn, w2_ref[...]) + b2_ref[...]
    o_ref[...] = x + gt_ref[pl.ds(b, 1), :] * y2


def _conv_sublayer(x, mods, layer, g, w_pw1, b_pw1, w_dw, b_dw, ln_g, ln_b, w_pw2, b_pw2,
                   *, seq, tm, cw, conv_rows=64):
    t, d = x.shape
    return pl.pallas_call(
        functools.partial(_conv_kernel, tiles_per_batch=seq // tm, cw=cw, conv_rows=conv_rows),
        grid=(t // tm,),
        in_specs=[
            pl.BlockSpec((tm, d), lambda i: (i, 0)),
            *_mod_specs(layer, 0, 1, d),
            _resident((1, d)),
            _resident((d, 2 * d)),
            _resident((1, 2 * d)),
            _resident((CONV_WIDTH, d)),
            _resident((1, d)),
            _resident((1, d)),
            _resident((1, d)),
            _resident((d, d)),
            _resident((1, d)),
        ],
        out_specs=pl.BlockSpec((tm, d), lambda i: (i, 0)),
        out_shape=jax.ShapeDtypeStruct((t, d), F32),
        scratch_shapes=[
            pltpu.VMEM((tm + CONV_HALO, d), F32),
        ],
        compiler_params=_compiler_params(1),
        name="conv_sublayer",
    )(x, mods, mods, mods, g, w_pw1, b_pw1, w_dw, b_dw, ln_g, ln_b, w_pw2, b_pw2)


def _pool_kernel(x_ref, sh_ref, sc_ref, gt_ref, g_ref, pw_ref, pb_ref, ls_ref, o_ref, hx_ref,
                 *, tiles_per_batch):
    i = pl.program_id(0)
    b = i // tiles_per_batch
    tile_in_seq = i % tiles_per_batch
    tm, d = x_ref.shape
    gd = d // len(POOL_WINDOWS)

    @pl.when(tile_in_seq == 0)
    def _():
        hx_ref[0:POOL_HALO, :] = jnp.zeros((POOL_HALO, d), F32)

    x = x_ref[...]
    hx_ref[POOL_HALO:, :] = _rms_modulate(x, g_ref[...], sh_ref[pl.ds(b, 1), :], sc_ref[pl.ds(b, 1), :])
    pos = tile_in_seq * tm + lax.broadcasted_iota(jnp.int32, (tm, 1), 0)
    gate = gt_ref[pl.ds(b, 1), :]

    for gi, win in enumerate(POOL_WINDOWS):
        cs = slice(gi * gd, (gi + 1) * gd)
        h = hx_ref[POOL_HALO:, cs]
        tot = h
        for back in range(1, win):
            tot = tot + hx_ref[POOL_HALO - back:POOL_HALO - back + tm, cs]
        count = jnp.minimum(pos + 1, win).astype(F32)
        p = (tot / count - h).astype(BF16)
        y = (_dot(p, pw_ref[gi]) + pb_ref[:, cs]) * ls_ref[:, cs]
        o_ref[:, cs] = x[:, cs] + gate[:, cs] * y

    hx_ref[0:POOL_HALO, :] = hx_ref[tm:tm + POOL_HALO, :]


def _pool_sublayer(x, mods, layer, g, pool_w, pool_b, pool_scale, *, seq, tm):
    t, d = x.shape
    groups, gd, _ = pool_w.shape
    return pl.pallas_call(
        functools.partial(_pool_kernel, tiles_per_batch=seq // tm),
        grid=(t // tm,),
        in_specs=[
            pl.BlockSpec((tm, d), lambda i: (i, 0)),
            *_mod_specs(layer, 0, 1, d),
            pl.BlockSpec((1, d), lambda i: (0, 0)),
            pl.BlockSpec((groups, gd, gd), lambda i: (0, 0, 0)),
            pl.BlockSpec((1, d), lambda i: (0, 0)),
            pl.BlockSpec((1, d), lambda i: (0, 0)),
        ],
        out_specs=pl.BlockSpec((tm, d), lambda i: (i, 0)),
        out_shape=jax.ShapeDtypeStruct((t, d), F32),
        scratch_shapes=[pltpu.VMEM((tm + POOL_HALO, d), F32)],
        compiler_params=_compiler_params(1),
        name="pool_sublayer",
    )(x, mods, mods, mods, g, pool_w, pool_b, pool_scale)


def kernel(x, c, w_ada, b_ada, norm_mix_g, norm_ffn_g,
           sgu_w_in, sgu_b_in, sgu_v_g, sgu_v_b, sgu_w_s, sgu_b_s, sgu_w_out, sgu_b_out,
           conv_w_pw1, conv_b_pw1, conv_w_dw, conv_b_dw, conv_ln_g, conv_ln_b, conv_w_pw2, conv_b_pw2,
           pool_w, pool_b, pool_scale, mlp_w1, mlp_w2, final_g):
    batch, seq, d = x.shape
    depth = w_ada.shape[0]
    row = lambda v: v.reshape(1, -1)

    mods = _modulation(c, w_ada, b_ada)
    xt = x.reshape(batch * seq, d)

    for layer in range(depth):
        kind, j = layer % N_MIXERS, layer // N_MIXERS
        g_mix = row(norm_mix_g[layer])
        if kind == 0:
            xt = _sgu_sublayer(
                xt, mods, layer, g_mix, sgu_w_in[j].astype(BF16), row(sgu_b_in[j]),
                row(sgu_v_g[j]), row(sgu_v_b[j]), sgu_w_s[j], sgu_b_s[j],
                sgu_w_out[j].astype(BF16), row(sgu_b_out[j]), seq=seq, tm=512, cw=512)
        elif kind == 1:
            xt = _conv_sublayer(
                xt, mods, layer, g_mix, conv_w_pw1[j].astype(BF16), row(conv_b_pw1[j]),
                conv_w_dw[j], row(conv_b_dw[j]), row(conv_ln_g[j]), row(conv_ln_b[j]),
                conv_w_pw2[j].astype(BF16), row(conv_b_pw2[j]), seq=seq, tm=512, cw=512)
        else:
            xt = _pool_sublayer(
                xt, mods, layer, g_mix, pool_w[j].astype(BF16), row(pool_b[j]), row(pool_scale[j]),
                seq=seq, tm=512)
        xt = _mlp_sublayer(
            xt, mods, layer, row(norm_ffn_g[layer]), mlp_w1[layer].astype(BF16),
            mlp_w2[layer].astype(BF16), row(final_g) if layer == depth - 1 else None,
            seq=seq, tm=512, tf=1024)

    return xt.reshape(batch, seq, d)
```

```python
import functools

import jax
import jax.numpy as jnp
from jax import lax
from jax.experimental import pallas as pl
from jax.experimental.pallas import tpu as pltpu

EPS = 1e-6
N_MIXERS = 3
SGU_CHUNK = 128
CONV_WIDTH = 31
CONV_HALO = 32
POOL_WINDOWS = (2, 4, 8, 16)
POOL_HALO = 16
SUBLANES, LANES = 8, 128
MOD_ROWS = SUBLANES

V7X_VMEM_BYTES = 64 * 1024 * 1024
VMEM_LIMIT_BYTES = V7X_VMEM_BYTES - 4 * 1024 * 1024

F32 = jnp.float32
BF16 = jnp.bfloat16


def _compiler_params(n_axes):
    return pltpu.CompilerParams(
        dimension_semantics=("arbitrary",) * n_axes,
        vmem_limit_bytes=VMEM_LIMIT_BYTES,
    )


def _rms_modulate(x, g, shift, scale):
    ms = jnp.mean(x * x, axis=-1, keepdims=True)
    y = x * lax.rsqrt(ms + EPS)
    return (y * g) * (1.0 + scale) + shift


def _layernorm(v, g, b):
    mu = jnp.mean(v, axis=-1, keepdims=True)
    d = v - mu
    var = jnp.mean(d * d, axis=-1, keepdims=True)
    return d * lax.rsqrt(var + EPS) * g + b


def _dot(a, b):
    return jnp.dot(a, b, preferred_element_type=F32)


def _mod_kernel(c_ref, w_ref, b_ref, o_ref):
    c = c_ref[...]
    c_act = (c * jax.nn.sigmoid(c)).astype(BF16)
    o_ref[...] = _dot(c_act, w_ref[...].astype(BF16)) + b_ref[...]


def _modulation(c, w_ada, b_ada, tn=1024):
    depth, d, n = w_ada.shape
    c_pad = jnp.pad(c, ((0, MOD_ROWS - c.shape[0]), (0, 0)))
    return pl.pallas_call(
        _mod_kernel,
        grid=(depth, n // tn),
        in_specs=[
            pl.BlockSpec((MOD_ROWS, d), lambda l, j: (0, 0)),
            pl.BlockSpec((None, d, tn), lambda l, j: (l, 0, j)),
            pl.BlockSpec((None, 1, tn), lambda l, j: (l, 0, j)),
        ],
        out_specs=pl.BlockSpec((None, MOD_ROWS, tn), lambda l, j: (l, 0, j)),
        out_shape=jax.ShapeDtypeStruct((depth, MOD_ROWS, n), F32),
        compiler_params=_compiler_params(2),
        name="adaln_modulation",
    )(c_pad, w_ada, b_ada.reshape(depth, 1, n))


def _mod_specs(layer, first, n_grid_axes, d):
    def spec(k):
        if n_grid_axes == 1:
            return pl.BlockSpec((None, MOD_ROWS, d), lambda i: (layer, 0, k))
        return pl.BlockSpec((None, MOD_ROWS, d), lambda i, s: (layer, 0, k))
    return [spec(first), spec(first + 1), spec(first + 2)]


def _mlp_kernel(x_ref, xn_ref, sh_ref, sc_ref, gt_ref, g_ref, w1_ref, w2_ref, *rest,
                tiles_per_batch, n_tiles, n_ff_steps, final_norm):
    if final_norm:
        fg_ref, o_ref, *h_refs = rest
    else:
        o_ref, *h_refs = rest
    i = pl.program_id(0)
    f = pl.program_id(1)
    b = i // tiles_per_batch
    tm = x_ref.shape[0]
    rows = tm // n_ff_steps

    def modulated(x, batch):
        h = _rms_modulate(x, g_ref[...], sh_ref[pl.ds(batch, 1), :], sc_ref[pl.ds(batch, 1), :])
        return h.astype(BF16)

    @pl.when(jnp.logical_and(i == 0, f == 0))
    def _():
        h_refs[0][...] = modulated(x_ref[...], b)

    next_rows = pl.ds(pl.multiple_of(f * rows, rows), rows)
    next_batch = jnp.minimum(i + 1, n_tiles - 1) // tiles_per_batch

    for parity in range(2):
        for first_step, base_ref in ((True, x_ref), (False, o_ref)):
            @pl.when(jnp.logical_and(i % 2 == parity, (f == 0) == first_step))
            def _(h_ref=h_refs[parity], hn_ref=h_refs[1 - parity], base_ref=base_ref):
                hn_ref[next_rows, :] = modulated(xn_ref[next_rows, :], next_batch)
                u = jnp.maximum(_dot(h_ref[...], w1_ref[...]), 0.0)
                u = (u * u).astype(BF16)
                o_ref[...] = base_ref[...] + gt_ref[pl.ds(b, 1), :] * _dot(u, w2_ref[...])

    if final_norm:
        @pl.when(f == n_ff_steps - 1)
        def _():
            o = o_ref[...]
            ms = jnp.mean(o * o, axis=-1, keepdims=True)
            o_ref[...] = o * lax.rsqrt(ms + EPS) * fg_ref[...]


def _mlp_sublayer(x, mods, layer, g, w1, w2, final_g, *, seq, tm, tf):
    t, d = x.shape
    ff = w1.shape[2]
    n_ff_steps = ff // tf
    final_norm = final_g is not None
    n_tiles = t // tm
    in_specs = [
        pl.BlockSpec((tm, d), lambda i, f: (i, 0)),
        pl.BlockSpec((tm, d), lambda i, f: (jnp.minimum(i + 1, n_tiles - 1), 0)),
        *_mod_specs(layer, 3, 2, d),
        pl.BlockSpec((1, d), lambda i, f: (0, 0)),
        pl.BlockSpec((None, d, tf), lambda i, f: (layer, 0, f)),
        pl.BlockSpec((None, tf, d), lambda i, f: (layer, f, 0)),
    ]
    args = [x, x, mods, mods, mods, g, w1, w2]
    if final_norm:
        in_specs.append(pl.BlockSpec((1, d), lambda i, f: (0, 0)))
        args.append(final_g)
    return pl.pallas_call(
        functools.partial(_mlp_kernel, tiles_per_batch=seq // tm, n_tiles=n_tiles, n_ff_steps=n_ff_steps,
                          final_norm=final_norm),
        grid=(n_tiles, n_ff_steps),
        in_specs=in_specs,
        out_specs=pl.BlockSpec((tm, d), lambda i, f: (i, 0)),
        out_shape=jax.ShapeDtypeStruct((t, d), F32),
        scratch_shapes=[pltpu.VMEM((tm, d), BF16), pltpu.VMEM((tm, d), BF16)],
        compiler_params=_compiler_params(2),
        name="mlp_sublayer",
    )(*args)


def _resident(shape, stack_index=None):
    if stack_index is None:
        return pl.BlockSpec(shape, lambda i: (0,) * len(shape), pipeline_mode=pl.Buffered(1))
    return pl.BlockSpec((None, *shape), lambda i: (stack_index,) + (0,) * len(shape),
                        pipeline_mode=pl.Buffered(1))


def _sgu_kernel(x_ref, sh_ref, sc_ref, gt_ref, g_ref, win_ref, bin_ref, vg_ref, vb_ref,
                ws_ref, bs_ref, wout_ref, bout_ref, o_ref, *, tiles_per_batch, cw):
    b = pl.program_id(0) // tiles_per_batch
    tm = x_ref.shape[0]
    width = wout_ref.shape[0]
    n_groups = width // cw

    x = x_ref[...]
    h = _rms_modulate(x, g_ref[...], sh_ref[pl.ds(b, 1), :], sc_ref[pl.ds(b, 1), :]).astype(BF16)

    def in_proj(col0):
        cs = slice(col0, col0 + cw)
        return jax.nn.gelu(_dot(h, win_ref[:, cs]) + bin_ref[:, cs])

    v = jnp.concatenate([in_proj(width + j * cw) for j in range(n_groups)], axis=-1)
    vn = _layernorm(v, vg_ref[...], vb_ref[...]).astype(BF16)

    rows = lax.broadcasted_iota(jnp.int32, (SGU_CHUNK, SGU_CHUNK), 0)
    cols = lax.broadcasted_iota(jnp.int32, (SGU_CHUNK, SGU_CHUNK), 1)
    causal = rows >= cols

    gated = []
    for j in range(n_groups):
        u = in_proj(j * cw)
        sv_heads = []
        for hh in range(j * cw // SGU_CHUNK, (j + 1) * cw // SGU_CHUNK):
            hs = slice(hh * SGU_CHUNK, (hh + 1) * SGU_CHUNK)
            w = jnp.where(causal, ws_ref[hh], 0.0).astype(BF16)
            bias = bs_ref[hh]
            sv_heads.append(jnp.concatenate(
                [_dot(w, vn[n * SGU_CHUNK:(n + 1) * SGU_CHUNK, hs]) + bias for n in range(tm // SGU_CHUNK)],
                axis=0))
        gated.append((u * jnp.concatenate(sv_heads, axis=-1)).astype(BF16))

    y = _dot(jnp.concatenate(gated, axis=-1), wout_ref[...]) + bout_ref[...]
    o_ref[...] = x + gt_ref[pl.ds(b, 1), :] * y


def _sgu_sublayer(x, mods, layer, j, g, w_in, b_in, v_g, v_b, w_s, b_s, w_out, b_out, *, seq, tm, cw):
    t, d = x.shape
    width = w_out.shape[1]
    heads = w_s.shape[0]
    return pl.pallas_call(
        functools.partial(_sgu_kernel, tiles_per_batch=seq // tm, cw=cw),
        grid=(t // tm,),
        in_specs=[
            pl.BlockSpec((tm, d), lambda i: (i, 0)),
            *_mod_specs(layer, 0, 1, d),
            _resident((1, d)),
            _resident((d, 2 * width), j),
            _resident((1, 2 * width)),
            _resident((1, width)),
            _resident((1, width)),
            _resident((heads, SGU_CHUNK, SGU_CHUNK)),
            _resident((heads, SGU_CHUNK, 1)),
            _resident((width, d), j),
            _resident((1, d)),
        ],
        out_specs=pl.BlockSpec((tm, d), lambda i: (i, 0)),
        out_shape=jax.ShapeDtypeStruct((t, d), F32),
        compiler_params=_compiler_params(1),
        name="sgu_sublayer",
    )(x, mods, mods, mods, g, w_in, b_in, v_g, v_b, w_s, b_s.reshape(heads, SGU_CHUNK, 1), w_out, b_out)


def _depthwise_causal_conv(a_ext, w, bias, tm):
    first_tap = CONV_HALO - (CONV_WIDTH - 1)
    total = None
    for r in range(SUBLANES):
        shift = first_tap + r
        lo0, shift = (shift // SUBLANES) * SUBLANES, shift % SUBLANES
        rows = tm + (SUBLANES if shift else 0)
        part = None
        for k in range(r, CONV_WIDTH, SUBLANES):
            lo = lo0 + k - r
            term = w[k:k + 1, :] * a_ext[lo:lo + rows, :]
            part = term if part is None else part + term
        part = part[shift:shift + tm, :]
        total = part if total is None else total + part
    return total + bias


def _conv_kernel(x_ref, sh_ref, sc_ref, gt_ref, g_ref, w1_ref, b1_ref, wdw_ref, bdw_ref, lng_ref, lnb_ref,
                 w2_ref, b2_ref, o_ref, halo_ref, *, tiles_per_batch, cw):
    i = pl.program_id(0)
    b = i // tiles_per_batch
    tm, d = x_ref.shape

    @pl.when(i % tiles_per_batch == 0)
    def _():
        halo_ref[...] = jnp.zeros((CONV_HALO, d), F32)

    x = x_ref[...]
    h = _rms_modulate(x, g_ref[...], sh_ref[pl.ds(b, 1), :], sc_ref[pl.ds(b, 1), :]).astype(BF16)

    ys = []
    for j in range(d // cw):
        cs = slice(j * cw, (j + 1) * cw)
        gs = slice(d + j * cw, d + (j + 1) * cw)
        val = _dot(h, w1_ref[:, cs]) + b1_ref[:, cs]
        gat = _dot(h, w1_ref[:, gs]) + b1_ref[:, gs]
        a = val * jax.nn.sigmoid(gat)
        a_ext = jnp.concatenate([halo_ref[:, cs], a], axis=0)
        halo_ref[:, cs] = a[tm - CONV_HALO:, :]
        ys.append(_depthwise_causal_conv(a_ext, wdw_ref[:, cs], bdw_ref[:, cs], tm))

    y = _layernorm(jnp.concatenate(ys, axis=-1), lng_ref[...], lnb_ref[...])
    yn = (y * jax.nn.sigmoid(y)).astype(BF16)
    y2 = _dot(yn, w2_ref[...]) + b2_ref[...]
    o_ref[...] = x + gt_ref[pl.ds(b, 1), :] * y2


def _conv_sublayer(x, mods, layer, j, g, w_pw1, b_pw1, w_dw, b_dw, ln_g, ln_b, w_pw2, b_pw2,
                   *, seq, tm, cw):
    t, d = x.shape
    return pl.pallas_call(
        functools.partial(_conv_kernel, tiles_per_batch=seq // tm, cw=cw),
        grid=(t // tm,),
        in_specs=[
            pl.BlockSpec((tm, d), lambda i: (i, 0)),
            *_mod_specs(layer, 0, 1, d),
            _resident((1, d)),
            _resident((d, 2 * d), j),
            _resident((1, 2 * d)),
            _resident((CONV_WIDTH, d)),
            _resident((1, d)),
            _resident((1, d)),
            _resident((1, d)),
            _resident((d, d), j),
            _resident((1, d)),
        ],
        out_specs=pl.BlockSpec((tm, d), lambda i: (i, 0)),
        out_shape=jax.ShapeDtypeStruct((t, d), F32),
        scratch_shapes=[pltpu.VMEM((CONV_HALO, d), F32)],
        compiler_params=_compiler_params(1),
        name="conv_sublayer",
    )(x, mods, mods, mods, g, w_pw1, b_pw1, w_dw, b_dw, ln_g, ln_b, w_pw2, b_pw2)


def _pool_kernel(x_ref, sh_ref, sc_ref, gt_ref, g_ref, pw_ref, pb_ref, ls_ref, o_ref, hx_ref,
                 *, tiles_per_batch):
    i = pl.program_id(0)
    b = i // tiles_per_batch
    tile_in_seq = i % tiles_per_batch
    tm, d = x_ref.shape
    gd = d // len(POOL_WINDOWS)

    @pl.when(tile_in_seq == 0)
    def _():
        hx_ref[0:POOL_HALO, :] = jnp.zeros((POOL_HALO, d), F32)

    x = x_ref[...]
    hx_ref[POOL_HALO:, :] = _rms_modulate(x, g_ref[...], sh_ref[pl.ds(b, 1), :], sc_ref[pl.ds(b, 1), :])
    pos = tile_in_seq * tm + lax.broadcasted_iota(jnp.int32, (tm, 1), 0)
    gate = gt_ref[pl.ds(b, 1), :]

    for gi, win in enumerate(POOL_WINDOWS):
        cs = slice(gi * gd, (gi + 1) * gd)
        h = hx_ref[POOL_HALO:, cs]
        tot = h
        for back in range(1, win):
            tot = tot + hx_ref[POOL_HALO - back:POOL_HALO - back + tm, cs]
        count = jnp.minimum(pos + 1, win).astype(F32)
        p = (tot / count - h).astype(BF16)
        y = (_dot(p, pw_ref[gi]) + pb_ref[:, cs]) * ls_ref[:, cs]
        o_ref[:, cs] = x[:, cs] + gate[:, cs] * y

    hx_ref[0:POOL_HALO, :] = hx_ref[tm:tm + POOL_HALO, :]


def _pool_sublayer(x, mods, layer, g, pool_w, pool_b, pool_scale, *, seq, tm):
    t, d = x.shape
    groups, gd, _ = pool_w.shape
    return pl.pallas_call(
        functools.partial(_pool_kernel, tiles_per_batch=seq // tm),
        grid=(t // tm,),
        in_specs=[
            pl.BlockSpec((tm, d), lambda i: (i, 0)),
            *_mod_specs(layer, 0, 1, d),
            pl.BlockSpec((1, d), lambda i: (0, 0)),
            pl.BlockSpec((groups, gd, gd), lambda i: (0, 0, 0)),
            pl.BlockSpec((1, d), lambda i: (0, 0)),
            pl.BlockSpec((1, d), lambda i: (0, 0)),
        ],
        out_specs=pl.BlockSpec((tm, d), lambda i: (i, 0)),
        out_shape=jax.ShapeDtypeStruct((t, d), F32),
        scratch_shapes=[pltpu.VMEM((tm + POOL_HALO, d), F32)],
        compiler_params=_compiler_params(1),
        name="pool_sublayer",
    )(x, mods, mods, mods, g, pool_w, pool_b, pool_scale)


def kernel(x, c, w_ada, b_ada, norm_mix_g, norm_ffn_g,
           sgu_w_in, sgu_b_in, sgu_v_g, sgu_v_b, sgu_w_s, sgu_b_s, sgu_w_out, sgu_b_out,
           conv_w_pw1, conv_b_pw1, conv_w_dw, conv_b_dw, conv_ln_g, conv_ln_b, conv_w_pw2, conv_b_pw2,
           pool_w, pool_b, pool_scale, mlp_w1, mlp_w2, final_g):
    batch, seq, d = x.shape
    depth = w_ada.shape[0]
    row = lambda v: v.reshape(1, -1)

    mods = _modulation(c, w_ada, b_ada)
    xt = x.reshape(batch * seq, d)
    sgu_w_in, sgu_w_out = sgu_w_in.astype(BF16), sgu_w_out.astype(BF16)
    conv_w_pw1, conv_w_pw2 = conv_w_pw1.astype(BF16), conv_w_pw2.astype(BF16)
    mlp_w1, mlp_w2 = mlp_w1.astype(BF16), mlp_w2.astype(BF16)

    for layer in range(depth):
        kind, j = layer % N_MIXERS, layer // N_MIXERS
        g_mix = row(norm_mix_g[layer])
        if kind == 0:
            xt = _sgu_sublayer(
                xt, mods, layer, j, g_mix, sgu_w_in, row(sgu_b_in[j]), row(sgu_v_g[j]), row(sgu_v_b[j]),
                sgu_w_s[j], sgu_b_s[j], sgu_w_out, row(sgu_b_out[j]), seq=seq, tm=512, cw=512)
        elif kind == 1:
            xt = _conv_sublayer(
                xt, mods, layer, j, g_mix, conv_w_pw1, row(conv_b_pw1[j]), conv_w_dw[j], row(conv_b_dw[j]),
                row(conv_ln_g[j]), row(conv_ln_b[j]), conv_w_pw2, row(conv_b_pw2[j]), seq=seq, tm=512, cw=256)
        else:
            xt = _pool_sublayer(
                xt, mods, layer, g_mix, pool_w[j].astype(BF16), row(pool_b[j]), row(pool_scale[j]),
                seq=seq, tm=512)
        xt = _mlp_sublayer(
            xt, mods, layer, row(norm_ffn_g[layer]), mlp_w1, mlp_w2,
            row(final_g) if layer == depth - 1 else None, seq=seq, tm=512, tf=1024)

    return xt.reshape(batch, seq, d)
```

```python
import functools

import jax
import jax.numpy as jnp
from jax import lax
from jax.experimental import pallas as pl
from jax.experimental.pallas import tpu as pltpu

EPS = 1e-6
N_MIXERS = 3
SGU_CHUNK = 128
CONV_WIDTH = 31
CONV_HALO = 32
POOL_WINDOWS = (2, 4, 8, 16)
POOL_HALO = 16
SUBLANES, LANES = 8, 128
MOD_ROWS = SUBLANES

V7X_VMEM_BYTES = 64 * 1024 * 1024
VMEM_LIMIT_BYTES = V7X_VMEM_BYTES - 2 * 1024 * 1024

F32 = jnp.float32
BF16 = jnp.bfloat16

MLP_TILES = (dict(tm=512, tf=1024), dict(tm=512, tf=2048), dict(tm=1024, tf=512), dict(tm=512, tf=1024))


def _compiler_params(n_axes):
    return pltpu.CompilerParams(
        dimension_semantics=("arbitrary",) * n_axes,
        vmem_limit_bytes=VMEM_LIMIT_BYTES,
    )


def _rms_modulate(x, g, shift, scale):
    ms = jnp.mean(x * x, axis=-1, keepdims=True)
    y = x * lax.rsqrt(ms + EPS)
    return (y * g) * (1.0 + scale) + shift


def _layernorm(v, g, b):
    mu = jnp.mean(v, axis=-1, keepdims=True)
    d = v - mu
    var = jnp.mean(d * d, axis=-1, keepdims=True)
    return d * lax.rsqrt(var + EPS) * g + b


def _dot(a, b):
    return jnp.dot(a, b, preferred_element_type=F32)


def _mod_kernel(c_ref, w_ref, b_ref, o_ref):
    c = c_ref[...]
    c_act = (c * jax.nn.sigmoid(c)).astype(BF16)
    o_ref[...] = _dot(c_act, w_ref[...].astype(BF16)) + b_ref[...]


def _modulation(c, w_ada, b_ada, tn=1024):
    depth, d, n = w_ada.shape
    c_pad = jnp.pad(c, ((0, MOD_ROWS - c.shape[0]), (0, 0)))
    return pl.pallas_call(
        _mod_kernel,
        grid=(depth, n // tn),
        in_specs=[
            pl.BlockSpec((MOD_ROWS, d), lambda l, j: (0, 0)),
            pl.BlockSpec((None, d, tn), lambda l, j: (l, 0, j)),
            pl.BlockSpec((None, 1, tn), lambda l, j: (l, 0, j)),
        ],
        out_specs=pl.BlockSpec((None, MOD_ROWS, tn), lambda l, j: (l, 0, j)),
        out_shape=jax.ShapeDtypeStruct((depth, MOD_ROWS, n), F32),
        compiler_params=_compiler_params(2),
        name="adaln_modulation",
    )(c_pad, w_ada, b_ada.reshape(depth, 1, n))


def _mod_specs(layer, first, n_grid_axes, d):
    def spec(k):
        if n_grid_axes == 1:
            return pl.BlockSpec((None, MOD_ROWS, d), lambda i: (layer, 0, k))
        return pl.BlockSpec((None, MOD_ROWS, d), lambda i, s: (layer, 0, k))
    return [spec(first), spec(first + 1), spec(first + 2)]


def _mlp_kernel(x0_ref, xs_ref, xns_ref, sh_ref, sc_ref, gt_ref, g_ref, w1_ref, w2_ref, *rest,
                tiles_per_batch, n_tiles, final_norm):
    if final_norm:
        fg_ref, o_ref, *h_refs = rest
    else:
        o_ref, *h_refs = rest
    i = pl.program_id(0)
    f = pl.program_id(1)
    n_ff_steps = pl.num_programs(1)
    b = i // tiles_per_batch
    rows = xs_ref.shape[0]
    step_rows = pl.ds(pl.multiple_of(f * rows, rows), rows)
    next_batch = jnp.minimum(i + 1, n_tiles - 1) // tiles_per_batch

    def modulated(x, batch):
        h = _rms_modulate(x, g_ref[...], sh_ref[pl.ds(batch, 1), :], sc_ref[pl.ds(batch, 1), :])
        return h.astype(BF16)

    @pl.when(jnp.logical_and(i == 0, f == 0))
    def _():
        h_refs[0][...] = modulated(x0_ref[...], b)

    for parity in range(2):
        for first_step in (True, False):
            @pl.when(jnp.logical_and(i % 2 == parity, (f == 0) == first_step))
            def _(h_ref=h_refs[parity], hn_ref=h_refs[1 - parity], first_step=first_step):
                hn_ref[step_rows, :] = modulated(xns_ref[...], next_batch)
                u = jnp.maximum(_dot(h_ref[...], w1_ref[...]), 0.0)
                u = (u * u).astype(BF16)
                y = gt_ref[pl.ds(b, 1), :] * _dot(u, w2_ref[...])
                o_ref[...] = y if first_step else o_ref[...] + y
                o_ref[step_rows, :] += xs_ref[...]

    if final_norm:
        @pl.when(f == n_ff_steps - 1)
        def _():
            o = o_ref[...]
            ms = jnp.mean(o * o, axis=-1, keepdims=True)
            o_ref[...] = o * lax.rsqrt(ms + EPS) * fg_ref[...]


def _mlp_sublayer(x, mods, layer, g, w1, w2, final_g, *, seq, tm, tf):
    t, d = x.shape
    ff = w1.shape[2]
    n_ff_steps = ff // tf
    final_norm = final_g is not None
    n_tiles = t // tm
    rows = tm // n_ff_steps
    in_specs = [
        pl.BlockSpec((tm, d), lambda i, f: (0, 0), pipeline_mode=pl.Buffered(1)),
        pl.BlockSpec((rows, d), lambda i, f: (i * n_ff_steps + f, 0)),
        pl.BlockSpec((rows, d), lambda i, f: (jnp.minimum(i + 1, n_tiles - 1) * n_ff_steps + f, 0)),
        *_mod_specs(layer, 3, 2, d),
        pl.BlockSpec((1, d), lambda i, f: (0, 0)),
        pl.BlockSpec((None, d, tf), lambda i, f: (layer, 0, f)),
        pl.BlockSpec((None, tf, d), lambda i, f: (layer, f, 0)),
    ]
    args = [x, x, x, mods, mods, mods, g, w1, w2]
    if final_norm:
        in_specs.append(pl.BlockSpec((1, d), lambda i, f: (0, 0)))
        args.append(final_g)
    return pl.pallas_call(
        functools.partial(_mlp_kernel, tiles_per_batch=seq // tm, n_tiles=n_tiles, final_norm=final_norm),
        grid=(n_tiles, n_ff_steps),
        in_specs=in_specs,
        out_specs=pl.BlockSpec((tm, d), lambda i, f: (i, 0)),
        out_shape=jax.ShapeDtypeStruct((t, d), F32),
        scratch_shapes=[pltpu.VMEM((tm, d), BF16), pltpu.VMEM((tm, d), BF16)],
        compiler_params=_compiler_params(2),
        name="mlp_sublayer",
    )(*args)


def _resident(shape, stack_index=None):
    if stack_index is None:
        return pl.BlockSpec(shape, lambda i: (0,) * len(shape), pipeline_mode=pl.Buffered(1))
    return pl.BlockSpec((None, *shape), lambda i: (stack_index,) + (0,) * len(shape),
                        pipeline_mode=pl.Buffered(1))


def _sgu_kernel(x_ref, sh_ref, sc_ref, gt_ref, g_ref, win_ref, bin_ref, vg_ref, vb_ref,
                ws_ref, bs_ref, wout_ref, bout_ref, o_ref, *, tiles_per_batch, cw):
    b = pl.program_id(0) // tiles_per_batch
    tm = x_ref.shape[0]
    width = wout_ref.shape[0]
    n_groups = width // cw

    x = x_ref[...]
    h = _rms_modulate(x, g_ref[...], sh_ref[pl.ds(b, 1), :], sc_ref[pl.ds(b, 1), :]).astype(BF16)

    def in_proj(col0):
        cs = slice(col0, col0 + cw)
        return jax.nn.gelu(_dot(h, win_ref[:, cs]) + bin_ref[:, cs])

    v = jnp.concatenate([in_proj(width + j * cw) for j in range(n_groups)], axis=-1)
    vn = _layernorm(v, vg_ref[...], vb_ref[...]).astype(BF16)

    rows = lax.broadcasted_iota(jnp.int32, (SGU_CHUNK, SGU_CHUNK), 0)
    cols = lax.broadcasted_iota(jnp.int32, (SGU_CHUNK, SGU_CHUNK), 1)
    causal = rows >= cols

    gated = []
    for j in range(n_groups):
        u = in_proj(j * cw)
        sv_heads = []
        for hh in range(j * cw // SGU_CHUNK, (j + 1) * cw // SGU_CHUNK):
            hs = slice(hh * SGU_CHUNK, (hh + 1) * SGU_CHUNK)
            w = jnp.where(causal, ws_ref[hh], 0.0).astype(BF16)
            bias = bs_ref[hh]
            sv_heads.append(jnp.concatenate(
                [_dot(w, vn[n * SGU_CHUNK:(n + 1) * SGU_CHUNK, hs]) + bias for n in range(tm // SGU_CHUNK)],
                axis=0))
        gated.append((u * jnp.concatenate(sv_heads, axis=-1)).astype(BF16))

    y = _dot(jnp.concatenate(gated, axis=-1), wout_ref[...]) + bout_ref[...]
    o_ref[...] = x + gt_ref[pl.ds(b, 1), :] * y


def _sgu_sublayer(x, mods, layer, j, g, w_in, b_in, v_g, v_b, w_s, b_s, w_out, b_out, *, seq, tm, cw):
    t, d = x.shape
    width = w_out.shape[1]
    heads = w_s.shape[0]
    return pl.pallas_call(
        functools.partial(_sgu_kernel, tiles_per_batch=seq // tm, cw=cw),
        grid=(t // tm,),
        in_specs=[
            pl.BlockSpec((tm, d), lambda i: (i, 0)),
            *_mod_specs(layer, 0, 1, d),
            _resident((1, d)),
            _resident((d, 2 * width), j),
            _resident((1, 2 * width)),
            _resident((1, width)),
            _resident((1, width)),
            _resident((heads, SGU_CHUNK, SGU_CHUNK)),
            _resident((heads, SGU_CHUNK, 1)),
            _resident((width, d), j),
            _resident((1, d)),
        ],
        out_specs=pl.BlockSpec((tm, d), lambda i: (i, 0)),
        out_shape=jax.ShapeDtypeStruct((t, d), F32),
        compiler_params=_compiler_params(1),
        name="sgu_sublayer",
    )(x, mods, mods, mods, g, w_in, b_in, v_g, v_b, w_s, b_s.reshape(heads, SGU_CHUNK, 1), w_out, b_out)


def _depthwise_causal_conv(a_ext, w, bias, tm):
    first_tap = CONV_HALO - (CONV_WIDTH - 1)
    total = None
    for r in range(SUBLANES):
        shift = first_tap + r
        lo0, shift = (shift // SUBLANES) * SUBLANES, shift % SUBLANES
        rows = tm + (SUBLANES if shift else 0)
        part = None
        for k in range(r, CONV_WIDTH, SUBLANES):
            lo = lo0 + k - r
            term = w[k:k + 1, :] * a_ext[lo:lo + rows, :]
            part = term if part is None else part + term
        part = part[shift:shift + tm, :]
        total = part if total is None else total + part
    return total + bias


def _conv_kernel(x_ref, sh_ref, sc_ref, gt_ref, g_ref, w1_ref, b1_ref, wdw_ref, bdw_ref, lng_ref, lnb_ref,
                 w2_ref, b2_ref, o_ref, halo_ref, *, tiles_per_batch, cw):
    i = pl.program_id(0)
    b = i // tiles_per_batch
    tm, d = x_ref.shape

    @pl.when(i % tiles_per_batch == 0)
    def _():
        halo_ref[...] = jnp.zeros((CONV_HALO, d), F32)

    x = x_ref[...]
    h = _rms_modulate(x, g_ref[...], sh_ref[pl.ds(b, 1), :], sc_ref[pl.ds(b, 1), :]).astype(BF16)

    ys = []
    for j in range(d // cw):
        cs = slice(j * cw, (j + 1) * cw)
        gs = slice(d + j * cw, d + (j + 1) * cw)
        val = _dot(h, w1_ref[:, cs]) + b1_ref[:, cs]
        gat = _dot(h, w1_ref[:, gs]) + b1_ref[:, gs]
        a = val * jax.nn.sigmoid(gat)
        a_ext = jnp.concatenate([halo_ref[:, cs], a], axis=0)
        halo_ref[:, cs] = a[tm - CONV_HALO:, :]
        ys.append(_depthwise_causal_conv(a_ext, wdw_ref[:, cs], bdw_ref[:, cs], tm))

    y = _layernorm(jnp.concatenate(ys, axis=-1), lng_ref[...], lnb_ref[...])
    yn = (y * jax.nn.sigmoid(y)).astype(BF16)
    y2 = _dot(yn, w2_ref[...]) + b2_ref[...]
    o_ref[...] = x + gt_ref[pl.ds(b, 1), :] * y2


def _conv_sublayer(x, mods, layer, j, g, w_pw1, b_pw1, w_dw, b_dw, ln_g, ln_b, w_pw2, b_pw2,
                   *, seq, tm, cw):
    t, d = x.shape
    return pl.pallas_call(
        functools.partial(_conv_kernel, tiles_per_batch=seq // tm, cw=cw),
        grid=(t // tm,),
        in_specs=[
            pl.BlockSpec((tm, d), lambda i: (i, 0)),
            *_mod_specs(layer, 0, 1, d),
            _resident((1, d)),
            _resident((d, 2 * d), j),
            _resident((1, 2 * d)),
            _resident((CONV_WIDTH, d)),
            _resident((1, d)),
            _resident((1, d)),
            _resident((1, d)),
            _resident((d, d), j),
            _resident((1, d)),
        ],
        out_specs=pl.BlockSpec((tm, d), lambda i: (i, 0)),
        out_shape=jax.ShapeDtypeStruct((t, d), F32),
        scratch_shapes=[pltpu.VMEM((CONV_HALO, d), F32)],
        compiler_params=_compiler_params(1),
        name="conv_sublayer",
    )(x, mods, mods, mods, g, w_pw1, b_pw1, w_dw, b_dw, ln_g, ln_b, w_pw2, b_pw2)


def _pool_kernel(x_ref, sh_ref, sc_ref, gt_ref, g_ref, pw_ref, pb_ref, ls_ref, o_ref, hx_ref,
                 *, tiles_per_batch):
    i = pl.program_id(0)
    b = i // tiles_per_batch
    tile_in_seq = i % tiles_per_batch
    tm, d = x_ref.shape
    gd = d // len(POOL_WINDOWS)

    @pl.when(tile_in_seq == 0)
    def _():
        hx_ref[0:POOL_HALO, :] = jnp.zeros((POOL_HALO, d), F32)

    x = x_ref[...]
    hx_ref[POOL_HALO:, :] = _rms_modulate(x, g_ref[...], sh_ref[pl.ds(b, 1), :], sc_ref[pl.ds(b, 1), :])
    pos = tile_in_seq * tm + lax.broadcasted_iota(jnp.int32, (tm, 1), 0)
    gate = gt_ref[pl.ds(b, 1), :]

    for gi, win in enumerate(POOL_WINDOWS):
        cs = slice(gi * gd, (gi + 1) * gd)
        h = hx_ref[POOL_HALO:, cs]
        tot = h
        for back in range(1, win):
            tot = tot + hx_ref[POOL_HALO - back:POOL_HALO - back + tm, cs]
        count = jnp.minimum(pos + 1, win).astype(F32)
        p = (tot / count - h).astype(BF16)
        y = (_dot(p, pw_ref[gi]) + pb_ref[:, cs]) * ls_ref[:, cs]
        o_ref[:, cs] = x[:, cs] + gate[:, cs] * y

    hx_ref[0:POOL_HALO, :] = hx_ref[tm:tm + POOL_HALO, :]


def _pool_sublayer(x, mods, layer, g, pool_w, pool_b, pool_scale, *, seq, tm):
    t, d = x.shape
    groups, gd, _ = pool_w.shape
    return pl.pallas_call(
        functools.partial(_pool_kernel, tiles_per_batch=seq // tm),
        grid=(t // tm,),
        in_specs=[
            pl.BlockSpec((tm, d), lambda i: (i, 0)),
            *_mod_specs(layer, 0, 1, d),
            pl.BlockSpec((1, d), lambda i: (0, 0)),
            pl.BlockSpec((groups, gd, gd), lambda i: (0, 0, 0)),
            pl.BlockSpec((1, d), lambda i: (0, 0)),
            pl.BlockSpec((1, d), lambda i: (0, 0)),
        ],
        out_specs=pl.BlockSpec((tm, d), lambda i: (i, 0)),
        out_shape=jax.ShapeDtypeStruct((t, d), F32),
        scratch_shapes=[pltpu.VMEM((tm + POOL_HALO, d), F32)],
        compiler_params=_compiler_params(1),
        name="pool_sublayer",
    )(x, mods, mods, mods, g, pool_w, pool_b, pool_scale)


def kernel(x, c, w_ada, b_ada, norm_mix_g, norm_ffn_g,
           sgu_w_in, sgu_b_in, sgu_v_g, sgu_v_b, sgu_w_s, sgu_b_s, sgu_w_out, sgu_b_out,
           conv_w_pw1, conv_b_pw1, conv_w_dw, conv_b_dw, conv_ln_g, conv_ln_b, conv_w_pw2, conv_b_pw2,
           pool_w, pool_b, pool_scale, mlp_w1, mlp_w2, final_g):
    batch, seq, d = x.shape
    depth = w_ada.shape[0]
    row = lambda v: v.reshape(1, -1)

    mods = _modulation(c, w_ada, b_ada)
    xt = x.reshape(batch * seq, d)
    sgu_w_in, sgu_w_out = sgu_w_in.astype(BF16), sgu_w_out.astype(BF16)
    conv_w_pw1, conv_w_pw2 = conv_w_pw1.astype(BF16), conv_w_pw2.astype(BF16)
    mlp_w1, mlp_w2 = mlp_w1.astype(BF16), mlp_w2.astype(BF16)

    for layer in range(depth):
        kind, j = layer % N_MIXERS, layer // N_MIXERS
        g_mix = row(norm_mix_g[layer])
        if kind == 0:
            xt = _sgu_sublayer(
                xt, mods, layer, j, g_mix, sgu_w_in, row(sgu_b_in[j]), row(sgu_v_g[j]), row(sgu_v_b[j]),
                sgu_w_s[j], sgu_b_s[j], sgu_w_out, row(sgu_b_out[j]), seq=seq, tm=512, cw=512)
        elif kind == 1:
            xt = _conv_sublayer(
                xt, mods, layer, j, g_mix, conv_w_pw1, row(conv_b_pw1[j]), conv_w_dw[j], row(conv_b_dw[j]),
                row(conv_ln_g[j]), row(conv_ln_b[j]), conv_w_pw2, row(conv_b_pw2[j]), seq=seq, tm=512, cw=256)
        else:
            xt = _pool_sublayer(
                xt, mods, layer, g_mix, pool_w[j].astype(BF16), row(pool_b[j]), row(pool_scale[j]),
                seq=seq, tm=512)
        xt = _mlp_sublayer(
            xt, mods, layer, row(norm_ffn_g[layer]), mlp_w1, mlp_w2,
            row(final_g) if layer == depth - 1 else None, seq=seq,
            **MLP_TILES[layer % len(MLP_TILES)])

    return xt.reshape(batch, seq, d)
```

```python
import functools

import jax
import jax.numpy as jnp
from jax import lax
from jax.experimental import pallas as pl
from jax.experimental.pallas import tpu as pltpu

EPS = 1e-6
N_MIXERS = 3
SGU_CHUNK = 128
CONV_WIDTH = 31
CONV_HALO = 32
POOL_WINDOWS = (2, 4, 8, 16)
POOL_HALO = 16
SUBLANES, LANES = 8, 128
MOD_ROWS = SUBLANES

V7X_VMEM_BYTES = 64 * 1024 * 1024
VMEM_LIMIT_BYTES = V7X_VMEM_BYTES - 2 * 1024 * 1024

F32 = jnp.float32
BF16 = jnp.bfloat16

MLP_TILES = dict(tm=512, tf=2048)
MIXER_TOKEN_TILE = 512
SGU_COLUMN_GROUP = 512
CONV_COLUMN_GROUP = 256
ADALN_COLUMN_TILE = 2048


def _compiler_params(n_axes):
    return pltpu.CompilerParams(
        dimension_semantics=("arbitrary",) * n_axes,
        vmem_limit_bytes=VMEM_LIMIT_BYTES,
    )


def _rms_modulate(x, g, shift, scale):
    ms = jnp.mean(x * x, axis=-1, keepdims=True)
    y = x * lax.rsqrt(ms + EPS)
    return (y * g) * (1.0 + scale) + shift


def _layernorm(v, g, b):
    mu = jnp.mean(v, axis=-1, keepdims=True)
    d = v - mu
    var = jnp.mean(d * d, axis=-1, keepdims=True)
    return d * lax.rsqrt(var + EPS) * g + b


def _dot(a, b):
    return jnp.dot(a, b, preferred_element_type=F32)


def _mod_kernel(c_ref, w_ref, b_ref, o_ref):
    c = c_ref[...]
    c_act = (c * jax.nn.sigmoid(c)).astype(BF16)
    o_ref[...] = _dot(c_act, w_ref[...].astype(BF16)) + b_ref[...]


def _modulation(c, w_ada, b_ada, tn=ADALN_COLUMN_TILE):
    depth, d, n = w_ada.shape
    c_pad = jnp.pad(c, ((0, MOD_ROWS - c.shape[0]), (0, 0)))
    return pl.pallas_call(
        _mod_kernel,
        grid=(depth, n // tn),
        in_specs=[
            pl.BlockSpec((MOD_ROWS, d), lambda l, j: (0, 0)),
            pl.BlockSpec((None, d, tn), lambda l, j: (l, 0, j)),
            pl.BlockSpec((None, 1, tn), lambda l, j: (l, 0, j)),
        ],
        out_specs=pl.BlockSpec((None, MOD_ROWS, tn), lambda l, j: (l, 0, j)),
        out_shape=jax.ShapeDtypeStruct((depth, MOD_ROWS, n), F32),
        compiler_params=_compiler_params(2),
        name="adaln_modulation",
    )(c_pad, w_ada, b_ada.reshape(depth, 1, n))


def _mod_specs(layer, first, n_grid_axes, d):
    def spec(k):
        if n_grid_axes == 1:
            return pl.BlockSpec((None, MOD_ROWS, d), lambda i: (layer, 0, k))
        return pl.BlockSpec((None, MOD_ROWS, d), lambda i, s: (layer, 0, k))
    return [spec(first), spec(first + 1), spec(first + 2)]


def _mlp_kernel(x0_ref, xs_ref, xns_ref, sh_ref, sc_ref, gt_ref, g_ref, w1_ref, w2_ref, *rest,
                tiles_per_batch, n_tiles, final_norm):
    if final_norm:
        fg_ref, o_ref, *h_refs = rest
    else:
        o_ref, *h_refs = rest
    i = pl.program_id(0)
    f = pl.program_id(1)
    n_ff_steps = pl.num_programs(1)
    b = i // tiles_per_batch
    rows = xs_ref.shape[0]
    step_rows = pl.ds(pl.multiple_of(f * rows, rows), rows)
    next_batch = jnp.minimum(i + 1, n_tiles - 1) // tiles_per_batch

    def modulated(x, batch):
        h = _rms_modulate(x, g_ref[...], sh_ref[pl.ds(batch, 1), :], sc_ref[pl.ds(batch, 1), :])
        return h.astype(BF16)

    @pl.when(jnp.logical_and(i == 0, f == 0))
    def _():
        h_refs[0][...] = modulated(x0_ref[...], b)

    for parity in range(2):
        for first_step in (True, False):
            @pl.when(jnp.logical_and(i % 2 == parity, (f == 0) == first_step))
            def _(h_ref=h_refs[parity], hn_ref=h_refs[1 - parity], first_step=first_step):
                hn_ref[step_rows, :] = modulated(xns_ref[...], next_batch)
                u = jnp.maximum(_dot(h_ref[...], w1_ref[...]), 0.0)
                u = (u * u).astype(BF16)
                y = gt_ref[pl.ds(b, 1), :] * _dot(u, w2_ref[...])
                o_ref[...] = y if first_step else o_ref[...] + y
                o_ref[step_rows, :] += xs_ref[...]

    if final_norm:
        @pl.when(f == n_ff_steps - 1)
        def _():
            o = o_ref[...]
            ms = jnp.mean(o * o, axis=-1, keepdims=True)
            o_ref[...] = o * lax.rsqrt(ms + EPS) * fg_ref[...]


def _mlp_sublayer(x, mods, layer, g, w1, w2, final_g, *, seq, tm, tf):
    t, d = x.shape
    ff = w1.shape[2]
    n_ff_steps = ff // tf
    final_norm = final_g is not None
    n_tiles = t // tm
    rows = tm // n_ff_steps
    in_specs = [
        pl.BlockSpec((tm, d), lambda i, f: (0, 0), pipeline_mode=pl.Buffered(1)),
        pl.BlockSpec((rows, d), lambda i, f: (i * n_ff_steps + f, 0)),
        pl.BlockSpec((rows, d), lambda i, f: (jnp.minimum(i + 1, n_tiles - 1) * n_ff_steps + f, 0)),
        *_mod_specs(layer, 3, 2, d),
        pl.BlockSpec((1, d), lambda i, f: (0, 0)),
        pl.BlockSpec((None, d, tf), lambda i, f: (layer, 0, f)),
        pl.BlockSpec((None, tf, d), lambda i, f: (layer, f, 0)),
    ]
    args = [x, x, x, mods, mods, mods, g, w1, w2]
    if final_norm:
        in_specs.append(pl.BlockSpec((1, d), lambda i, f: (0, 0)))
        args.append(final_g)
    return pl.pallas_call(
        functools.partial(_mlp_kernel, tiles_per_batch=seq // tm, n_tiles=n_tiles, final_norm=final_norm),
        grid=(n_tiles, n_ff_steps),
        in_specs=in_specs,
        out_specs=pl.BlockSpec((tm, d), lambda i, f: (i, 0)),
        out_shape=jax.ShapeDtypeStruct((t, d), F32),
        scratch_shapes=[pltpu.VMEM((tm, d), BF16), pltpu.VMEM((tm, d), BF16)],
        compiler_params=_compiler_params(2),
        name="mlp_sublayer",
    )(*args)


def _resident(shape, stack_index=None):
    if stack_index is None:
        return pl.BlockSpec(shape, lambda i: (0,) * len(shape), pipeline_mode=pl.Buffered(1))
    return pl.BlockSpec((None, *shape), lambda i: (stack_index,) + (0,) * len(shape),
                        pipeline_mode=pl.Buffered(1))


def _sgu_kernel(x_ref, sh_ref, sc_ref, gt_ref, g_ref, win_ref, bin_ref, vg_ref, vb_ref,
                ws_ref, bs_ref, wout_ref, bout_ref, o_ref, *, tiles_per_batch, cw):
    b = pl.program_id(0) // tiles_per_batch
    tm = x_ref.shape[0]
    width = wout_ref.shape[0]
    n_groups = width // cw

    x = x_ref[...]
    h = _rms_modulate(x, g_ref[...], sh_ref[pl.ds(b, 1), :], sc_ref[pl.ds(b, 1), :]).astype(BF16)

    def in_proj(col0):
        cs = slice(col0, col0 + cw)
        return jax.nn.gelu(_dot(h, win_ref[:, cs]) + bin_ref[:, cs])

    v = jnp.concatenate([in_proj(width + j * cw) for j in range(n_groups)], axis=-1)
    vn = _layernorm(v, vg_ref[...], vb_ref[...]).astype(BF16)

    rows = lax.broadcasted_iota(jnp.int32, (SGU_CHUNK, SGU_CHUNK), 0)
    cols = lax.broadcasted_iota(jnp.int32, (SGU_CHUNK, SGU_CHUNK), 1)
    causal = rows >= cols

    gated = []
    for j in range(n_groups):
        u = in_proj(j * cw)
        sv_heads = []
        for hh in range(j * cw // SGU_CHUNK, (j + 1) * cw // SGU_CHUNK):
            hs = slice(hh * SGU_CHUNK, (hh + 1) * SGU_CHUNK)
            w = jnp.where(causal, ws_ref[hh], 0.0).astype(BF16)
            bias = bs_ref[hh]
            sv_heads.append(jnp.concatenate(
                [_dot(w, vn[n * SGU_CHUNK:(n + 1) * SGU_CHUNK, hs]) + bias for n in range(tm // SGU_CHUNK)],
                axis=0))
        gated.append((u * jnp.concatenate(sv_heads, axis=-1)).astype(BF16))

    y = _dot(jnp.concatenate(gated, axis=-1), wout_ref[...]) + bout_ref[...]
    o_ref[...] = x + gt_ref[pl.ds(b, 1), :] * y


def _sgu_sublayer(x, mods, layer, j, g, w_in, b_in, v_g, v_b, w_s, b_s, w_out, b_out, *, seq, tm, cw):
    t, d = x.shape
    width = w_out.shape[1]
    heads = w_s.shape[0]
    return pl.pallas_call(
        functools.partial(_sgu_kernel, tiles_per_batch=seq // tm, cw=cw),
        grid=(t // tm,),
        in_specs=[
            pl.BlockSpec((tm, d), lambda i: (i, 0)),
            *_mod_specs(layer, 0, 1, d),
            _resident((1, d)),
            _resident((d, 2 * width), j),
            _resident((1, 2 * width)),
            _resident((1, width)),
            _resident((1, width)),
            _resident((heads, SGU_CHUNK, SGU_CHUNK)),
            _resident((heads, SGU_CHUNK, 1)),
            _resident((width, d), j),
            _resident((1, d)),
        ],
        out_specs=pl.BlockSpec((tm, d), lambda i: (i, 0)),
        out_shape=jax.ShapeDtypeStruct((t, d), F32),
        compiler_params=_compiler_params(1),
        name="sgu_sublayer",
    )(x, mods, mods, mods, g, w_in, b_in, v_g, v_b, w_s, b_s.reshape(heads, SGU_CHUNK, 1), w_out, b_out)


def _depthwise_causal_conv(a_ext, w, bias, tm):
    first_tap = CONV_HALO - (CONV_WIDTH - 1)
    total = None
    for r in range(SUBLANES):
        shift = first_tap + r
        lo0, shift = (shift // SUBLANES) * SUBLANES, shift % SUBLANES
        rows = tm + (SUBLANES if shift else 0)
        part = None
        for k in range(r, CONV_WIDTH, SUBLANES):
            lo = lo0 + k - r
            term = w[k:k + 1, :] * a_ext[lo:lo + rows, :]
            part = term if part is None else part + term
        part = part[shift:shift + tm, :]
        total = part if total is None else total + part
    return total + bias


def _conv_kernel(x_ref, sh_ref, sc_ref, gt_ref, g_ref, w1_ref, b1_ref, wdw_ref, bdw_ref, lng_ref, lnb_ref,
                 w2_ref, b2_ref, o_ref, halo_ref, *, tiles_per_batch, cw):
    i = pl.program_id(0)
    b = i // tiles_per_batch
    tm, d = x_ref.shape

    @pl.when(i % tiles_per_batch == 0)
    def _():
        halo_ref[...] = jnp.zeros((CONV_HALO, d), F32)

    x = x_ref[...]
    h = _rms_modulate(x, g_ref[...], sh_ref[pl.ds(b, 1), :], sc_ref[pl.ds(b, 1), :]).astype(BF16)

    ys = []
    for j in range(d // cw):
        cs = slice(j * cw, (j + 1) * cw)
        gs = slice(d + j * cw, d + (j + 1) * cw)
        val = _dot(h, w1_ref[:, cs]) + b1_ref[:, cs]
        gat = _dot(h, w1_ref[:, gs]) + b1_ref[:, gs]
        a = val * jax.nn.sigmoid(gat)
        a_ext = jnp.concatenate([halo_ref[:, cs], a], axis=0)
        halo_ref[:, cs] = a[tm - CONV_HALO:, :]
        ys.append(_depthwise_causal_conv(a_ext, wdw_ref[:, cs], bdw_ref[:, cs], tm))

    y = _layernorm(jnp.concatenate(ys, axis=-1), lng_ref[...], lnb_ref[...])
    yn = (y * jax.nn.sigmoid(y)).astype(BF16)
    y2 = _dot(yn, w2_ref[...]) + b2_ref[...]
    o_ref[...] = x + gt_ref[pl.ds(b, 1), :] * y2


def _conv_sublayer(x, mods, layer, j, g, w_pw1, b_pw1, w_dw, b_dw, ln_g, ln_b, w_pw2, b_pw2,
                   *, seq, tm, cw):
    t, d = x.shape
    return pl.pallas_call(
        functools.partial(_conv_kernel, tiles_per_batch=seq // tm, cw=cw),
        grid=(t // tm,),
        in_specs=[
            pl.BlockSpec((tm, d), lambda i: (i, 0)),
            *_mod_specs(layer, 0, 1, d),
            _resident((1, d)),
            _resident((d, 2 * d), j),
            _resident((1, 2 * d)),
            _resident((CONV_WIDTH, d)),
            _resident((1, d)),
            _resident((1, d)),
            _resident((1, d)),
            _resident((d, d), j),
            _resident((1, d)),
        ],
        out_specs=pl.BlockSpec((tm, d), lambda i: (i, 0)),
        out_shape=jax.ShapeDtypeStruct((t, d), F32),
        scratch_shapes=[pltpu.VMEM((CONV_HALO, d), F32)],
        compiler_params=_compiler_params(1),
        name="conv_sublayer",
    )(x, mods, mods, mods, g, w_pw1, b_pw1, w_dw, b_dw, ln_g, ln_b, w_pw2, b_pw2)


def _pool_kernel(x_ref, sh_ref, sc_ref, gt_ref, g_ref, pw_ref, pb_ref, ls_ref, o_ref, hx_ref,
                 *, tiles_per_batch):
    i = pl.program_id(0)
    b = i // tiles_per_batch
    tile_in_seq = i % tiles_per_batch
    tm, d = x_ref.shape
    gd = d // len(POOL_WINDOWS)

    @pl.when(tile_in_seq == 0)
    def _():
        hx_ref[0:POOL_HALO, :] = jnp.zeros((POOL_HALO, d), F32)

    x = x_ref[...]
    hx_ref[POOL_HALO:, :] = _rms_modulate(x, g_ref[...], sh_ref[pl.ds(b, 1), :], sc_ref[pl.ds(b, 1), :])
    pos = tile_in_seq * tm + lax.broadcasted_iota(jnp.int32, (tm, 1), 0)
    gate = gt_ref[pl.ds(b, 1), :]

    for gi, win in enumerate(POOL_WINDOWS):
        cs = slice(gi * gd, (gi + 1) * gd)
        h = hx_ref[POOL_HALO:, cs]
        tot = h
        for back in range(1, win):
            tot = tot + hx_ref[POOL_HALO - back:POOL_HALO - back + tm, cs]
        count = jnp.minimum(pos + 1, win).astype(F32)
        p = (tot / count - h).astype(BF16)
        y = (_dot(p, pw_ref[gi]) + pb_ref[:, cs]) * ls_ref[:, cs]
        o_ref[:, cs] = x[:, cs] + gate[:, cs] * y

    hx_ref[0:POOL_HALO, :] = hx_ref[tm:tm + POOL_HALO, :]


def _pool_sublayer(x, mods, layer, g, pool_w, pool_b, pool_scale, *, seq, tm):
    t, d = x.shape
    groups, gd, _ = pool_w.shape
    return pl.pallas_call(
        functools.partial(_pool_kernel, tiles_per_batch=seq // tm),
        grid=(t // tm,),
        in_specs=[
            pl.BlockSpec((tm, d), lambda i: (i, 0)),
            *_mod_specs(layer, 0, 1, d),
            pl.BlockSpec((1, d), lambda i: (0, 0)),
            pl.BlockSpec((groups, gd, gd), lambda i: (0, 0, 0)),
            pl.BlockSpec((1, d), lambda i: (0, 0)),
            pl.BlockSpec((1, d), lambda i: (0, 0)),
        ],
        out_specs=pl.BlockSpec((tm, d), lambda i: (i, 0)),
        out_shape=jax.ShapeDtypeStruct((t, d), F32),
        scratch_shapes=[pltpu.VMEM((tm + POOL_HALO, d), F32)],
        compiler_params=_compiler_params(1),
        name="pool_sublayer",
    )(x, mods, mods, mods, g, pool_w, pool_b, pool_scale)


def kernel(x, c, w_ada, b_ada, norm_mix_g, norm_ffn_g,
           sgu_w_in, sgu_b_in, sgu_v_g, sgu_v_b, sgu_w_s, sgu_b_s, sgu_w_out, sgu_b_out,
           conv_w_pw1, conv_b_pw1, conv_w_dw, conv_b_dw, conv_ln_g, conv_ln_b, conv_w_pw2, conv_b_pw2,
           pool_w, pool_b, pool_scale, mlp_w1, mlp_w2, final_g):
    batch, seq, d = x.shape
    depth = w_ada.shape[0]
    row = lambda v: v.reshape(1, -1)

    mods = _modulation(c, w_ada, b_ada)
    xt = x.reshape(batch * seq, d)
    sgu_w_in, sgu_w_out = sgu_w_in.astype(BF16), sgu_w_out.astype(BF16)
    conv_w_pw1, conv_w_pw2 = conv_w_pw1.astype(BF16), conv_w_pw2.astype(BF16)
    mlp_w1, mlp_w2 = mlp_w1.astype(BF16), mlp_w2.astype(BF16)

    for layer in range(depth):
        kind, j = layer % N_MIXERS, layer // N_MIXERS
        g_mix = row(norm_mix_g[layer])
        if kind == 0:
            xt = _sgu_sublayer(
                xt, mods, layer, j, g_mix, sgu_w_in, row(sgu_b_in[j]), row(sgu_v_g[j]), row(sgu_v_b[j]),
                sgu_w_s[j], sgu_b_s[j], sgu_w_out, row(sgu_b_out[j]), seq=seq, tm=MIXER_TOKEN_TILE,
                cw=SGU_COLUMN_GROUP)
        elif kind == 1:
            xt = _conv_sublayer(
                xt, mods, layer, j, g_mix, conv_w_pw1, row(conv_b_pw1[j]), conv_w_dw[j], row(conv_b_dw[j]),
                row(conv_ln_g[j]), row(conv_ln_b[j]), conv_w_pw2, row(conv_b_pw2[j]), seq=seq,
                tm=MIXER_TOKEN_TILE, cw=CONV_COLUMN_GROUP)
        else:
            xt = _pool_sublayer(
                xt, mods, layer, g_mix, pool_w[j].astype(BF16), row(pool_b[j]), row(pool_scale[j]),
                seq=seq, tm=MIXER_TOKEN_TILE)
        xt = _mlp_sublayer(
            xt, mods, layer, row(norm_ffn_g[layer]), mlp_w1, mlp_w2,
            row(final_g) if layer == depth - 1 else None, seq=seq, **MLP_TILES)

    return xt.reshape(batch, seq, d)
```

```python
import functools

import jax
import jax.numpy as jnp
from jax import lax
from jax.experimental import pallas as pl
from jax.experimental.pallas import tpu as pltpu

EPS = 1e-6
N_MIXERS = 3
SGU_CHUNK = 128
CONV_WIDTH = 31
CONV_HALO = 32
POOL_WINDOWS = (2, 4, 8, 16)
POOL_HALO = 16
SUBLANES, LANES = 8, 128
MOD_ROWS = SUBLANES

V7X_VMEM_BYTES = 64 * 1024 * 1024
VMEM_LIMIT_BYTES = V7X_VMEM_BYTES - 2 * 1024 * 1024

F32 = jnp.float32
BF16 = jnp.bfloat16

MLP_TILES = dict(tm=512, tf=2048)
MIXER_TOKEN_TILE = 512
SGU_COLUMN_GROUP = 256
CONV_COLUMN_GROUP = 512
ADALN_COLUMN_TILE = 2048


def _compiler_params(n_axes):
    return pltpu.CompilerParams(
        dimension_semantics=("arbitrary",) * n_axes,
        vmem_limit_bytes=VMEM_LIMIT_BYTES,
    )


def _rms_modulate(x, g, shift, scale):
    ms = jnp.mean(x * x, axis=-1, keepdims=True)
    y = x * lax.rsqrt(ms + EPS)
    return (y * g) * (1.0 + scale) + shift


def _layernorm(v, g, b):
    mu = jnp.mean(v, axis=-1, keepdims=True)
    d = v - mu
    var = jnp.mean(d * d, axis=-1, keepdims=True)
    return d * lax.rsqrt(var + EPS) * g + b


def _dot(a, b):
    return jnp.dot(a, b, preferred_element_type=F32)


def _mod_kernel(c_ref, w_ref, b_ref, o_ref):
    c = c_ref[...]
    c_act = (c * jax.nn.sigmoid(c)).astype(BF16)
    o_ref[...] = _dot(c_act, w_ref[...].astype(BF16)) + b_ref[...]


def _modulation(c, w_ada, b_ada, tn=ADALN_COLUMN_TILE):
    depth, d, n = w_ada.shape
    c_pad = jnp.pad(c, ((0, MOD_ROWS - c.shape[0]), (0, 0)))
    return pl.pallas_call(
        _mod_kernel,
        grid=(depth, n // tn),
        in_specs=[
            pl.BlockSpec((MOD_ROWS, d), lambda l, j: (0, 0)),
            pl.BlockSpec((None, d, tn), lambda l, j: (l, 0, j)),
            pl.BlockSpec((None, 1, tn), lambda l, j: (l, 0, j)),
        ],
        out_specs=pl.BlockSpec((None, MOD_ROWS, tn), lambda l, j: (l, 0, j)),
        out_shape=jax.ShapeDtypeStruct((depth, MOD_ROWS, n), F32),
        compiler_params=_compiler_params(2),
        name="adaln_modulation",
    )(c_pad, w_ada, b_ada.reshape(depth, 1, n))


def _mod_specs(layer, first, n_grid_axes, d):
    def spec(k):
        if n_grid_axes == 1:
            return pl.BlockSpec((None, MOD_ROWS, d), lambda i: (layer, 0, k))
        return pl.BlockSpec((None, MOD_ROWS, d), lambda i, s: (layer, 0, k))
    return [spec(first), spec(first + 1), spec(first + 2)]


def _mlp_kernel(x0_ref, xs_ref, xns_ref, sh_ref, sc_ref, gt_ref, g_ref, w1_ref, w2_ref, *rest,
                tiles_per_batch, n_tiles, final_norm):
    if final_norm:
        fg_ref, o_ref, *h_refs = rest
    else:
        o_ref, *h_refs = rest
    i = pl.program_id(0)
    f = pl.program_id(1)
    n_ff_steps = pl.num_programs(1)
    b = i // tiles_per_batch
    rows = xs_ref.shape[0]
    step_rows = pl.ds(pl.multiple_of(f * rows, rows), rows)
    next_batch = jnp.minimum(i + 1, n_tiles - 1) // tiles_per_batch

    def modulated(x, batch):
        h = _rms_modulate(x, g_ref[...], sh_ref[pl.ds(batch, 1), :], sc_ref[pl.ds(batch, 1), :])
        return h.astype(BF16)

    @pl.when(jnp.logical_and(i == 0, f == 0))
    def _():
        h_refs[0][...] = modulated(x0_ref[...], b)

    for parity in range(2):
        for first_step in (True, False):
            @pl.when(jnp.logical_and(i % 2 == parity, (f == 0) == first_step))
            def _(h_ref=h_refs[parity], hn_ref=h_refs[1 - parity], first_step=first_step):
                hn_ref[step_rows, :] = modulated(xns_ref[...], next_batch)
                u = jnp.maximum(_dot(h_ref[...], w1_ref[...]), 0.0)
                u = (u * u).astype(BF16)
                y = gt_ref[pl.ds(b, 1), :] * _dot(u, w2_ref[...])
                o_ref[...] = y if first_step else o_ref[...] + y
                o_ref[step_rows, :] += xs_ref[...]

    if final_norm:
        @pl.when(f == n_ff_steps - 1)
        def _():
            o = o_ref[...]
            ms = jnp.mean(o * o, axis=-1, keepdims=True)
            o_ref[...] = o * lax.rsqrt(ms + EPS) * fg_ref[...]


def _mlp_sublayer(x, mods, layer, g, w1, w2, final_g, *, seq, tm, tf):
    t, d = x.shape
    ff = w1.shape[2]
    n_ff_steps = ff // tf
    final_norm = final_g is not None
    n_tiles = t // tm
    rows = tm // n_ff_steps
    in_specs = [
        pl.BlockSpec((tm, d), lambda i, f: (0, 0), pipeline_mode=pl.Buffered(1)),
        pl.BlockSpec((rows, d), lambda i, f: (i * n_ff_steps + f, 0)),
        pl.BlockSpec((rows, d), lambda i, f: (jnp.minimum(i + 1, n_tiles - 1) * n_ff_steps + f, 0)),
        *_mod_specs(layer, 3, 2, d),
        pl.BlockSpec((1, d), lambda i, f: (0, 0)),
        pl.BlockSpec((None, d, tf), lambda i, f: (layer, 0, f)),
        pl.BlockSpec((None, tf, d), lambda i, f: (layer, f, 0)),
    ]
    args = [x, x, x, mods, mods, mods, g, w1, w2]
    if final_norm:
        in_specs.append(pl.BlockSpec((1, d), lambda i, f: (0, 0)))
        args.append(final_g)
    return pl.pallas_call(
        functools.partial(_mlp_kernel, tiles_per_batch=seq // tm, n_tiles=n_tiles, final_norm=final_norm),
        grid=(n_tiles, n_ff_steps),
        in_specs=in_specs,
        out_specs=pl.BlockSpec((tm, d), lambda i, f: (i, 0)),
        out_shape=jax.ShapeDtypeStruct((t, d), F32),
        scratch_shapes=[pltpu.VMEM((tm, d), BF16), pltpu.VMEM((tm, d), BF16)],
        compiler_params=_compiler_params(2),
        name="mlp_sublayer",
    )(*args)


def _resident(shape, stack_index=None):
    if stack_index is None:
        return pl.BlockSpec(shape, lambda i: (0,) * len(shape), pipeline_mode=pl.Buffered(1))
    return pl.BlockSpec((None, *shape), lambda i: (stack_index,) + (0,) * len(shape),
                        pipeline_mode=pl.Buffered(1))


def _sgu_kernel(x_ref, sh_ref, sc_ref, gt_ref, g_ref, win_ref, bin_ref, vg_ref, vb_ref,
                ws_ref, bs_ref, wout_ref, bout_ref, o_ref, *, tiles_per_batch, cw):
    b = pl.program_id(0) // tiles_per_batch
    tm = x_ref.shape[0]
    width = wout_ref.shape[0]
    n_groups = width // cw

    x = x_ref[...]
    h = _rms_modulate(x, g_ref[...], sh_ref[pl.ds(b, 1), :], sc_ref[pl.ds(b, 1), :]).astype(BF16)

    def in_proj(col0):
        cs = slice(col0, col0 + cw)
        return jax.nn.gelu(_dot(h, win_ref[:, cs]) + bin_ref[:, cs])

    v = jnp.concatenate([in_proj(width + j * cw) for j in range(n_groups)], axis=-1)
    vn = _layernorm(v, vg_ref[...], vb_ref[...]).astype(BF16)

    rows = lax.broadcasted_iota(jnp.int32, (SGU_CHUNK, SGU_CHUNK), 0)
    cols = lax.broadcasted_iota(jnp.int32, (SGU_CHUNK, SGU_CHUNK), 1)
    causal = rows >= cols

    gated = []
    for j in range(n_groups):
        u = in_proj(j * cw)
        sv_heads = []
        for hh in range(j * cw // SGU_CHUNK, (j + 1) * cw // SGU_CHUNK):
            hs = slice(hh * SGU_CHUNK, (hh + 1) * SGU_CHUNK)
            w = jnp.where(causal, ws_ref[hh], 0.0).astype(BF16)
            bias = bs_ref[hh]
            sv_heads.append(jnp.concatenate(
                [_dot(w, vn[n * SGU_CHUNK:(n + 1) * SGU_CHUNK, hs]) + bias for n in range(tm // SGU_CHUNK)],
                axis=0))
        gated.append((u * jnp.concatenate(sv_heads, axis=-1)).astype(BF16))

    y = _dot(jnp.concatenate(gated, axis=-1), wout_ref[...]) + bout_ref[...]
    o_ref[...] = x + gt_ref[pl.ds(b, 1), :] * y


def _sgu_sublayer(x, mods, layer, j, g, w_in, b_in, v_g, v_b, w_s, b_s, w_out, b_out, *, seq, tm, cw):
    t, d = x.shape
    width = w_out.shape[1]
    heads = w_s.shape[0]
    return pl.pallas_call(
        functools.partial(_sgu_kernel, tiles_per_batch=seq // tm, cw=cw),
        grid=(t // tm,),
        in_specs=[
            pl.BlockSpec((tm, d), lambda i: (i, 0)),
            *_mod_specs(layer, 0, 1, d),
            _resident((1, d)),
            _resident((d, 2 * width), j),
            _resident((1, 2 * width)),
            _resident((1, width)),
            _resident((1, width)),
            _resident((heads, SGU_CHUNK, SGU_CHUNK)),
            _resident((heads, SGU_CHUNK, 1)),
            _resident((width, d), j),
            _resident((1, d)),
        ],
        out_specs=pl.BlockSpec((tm, d), lambda i: (i, 0)),
        out_shape=jax.ShapeDtypeStruct((t, d), F32),
        compiler_params=_compiler_params(1),
        name="sgu_sublayer",
    )(x, mods, mods, mods, g, w_in, b_in, v_g, v_b, w_s, b_s.reshape(heads, SGU_CHUNK, 1), w_out, b_out)


def _depthwise_causal_conv(a_ext, w, bias, tm):
    first_tap = CONV_HALO - (CONV_WIDTH - 1)
    total = None
    for r in range(SUBLANES):
        shift = first_tap + r
        lo0, shift = (shift // SUBLANES) * SUBLANES, shift % SUBLANES
        rows = tm + (SUBLANES if shift else 0)
        part = None
        for k in range(r, CONV_WIDTH, SUBLANES):
            lo = lo0 + k - r
            term = w[k:k + 1, :] * a_ext[lo:lo + rows, :]
            part = term if part is None else part + term
        part = part[shift:shift + tm, :]
        total = part if total is None else total + part
    return total + bias


def _conv_kernel(x_ref, sh_ref, sc_ref, gt_ref, g_ref, w1_ref, b1_ref, wdw_ref, bdw_ref, lng_ref, lnb_ref,
                 w2_ref, b2_ref, o_ref, halo_ref, *, tiles_per_batch, cw):
    i = pl.program_id(0)
    b = i // tiles_per_batch
    tm, d = x_ref.shape

    @pl.when(i % tiles_per_batch == 0)
    def _():
        halo_ref[...] = jnp.zeros((CONV_HALO, d), F32)

    x = x_ref[...]
    h = _rms_modulate(x, g_ref[...], sh_ref[pl.ds(b, 1), :], sc_ref[pl.ds(b, 1), :]).astype(BF16)

    ys = []
    for j in range(d // cw):
        cs = slice(j * cw, (j + 1) * cw)
        gs = slice(d + j * cw, d + (j + 1) * cw)
        val = _dot(h, w1_ref[:, cs]) + b1_ref[:, cs]
        gat = _dot(h, w1_ref[:, gs]) + b1_ref[:, gs]
        a = val * jax.nn.sigmoid(gat)
        a_ext = jnp.concatenate([halo_ref[:, cs], a], axis=0)
        halo_ref[:, cs] = a[tm - CONV_HALO:, :]
        ys.append(_depthwise_causal_conv(a_ext, wdw_ref[:, cs], bdw_ref[:, cs], tm))

    y = _layernorm(jnp.concatenate(ys, axis=-1), lng_ref[...], lnb_ref[...])
    yn = (y * jax.nn.sigmoid(y)).astype(BF16)
    y2 = _dot(yn, w2_ref[...]) + b2_ref[...]
    o_ref[...] = x + gt_ref[pl.ds(b, 1), :] * y2


def _conv_sublayer(x, mods, layer, j, g, w_pw1, b_pw1, w_dw, b_dw, ln_g, ln_b, w_pw2, b_pw2,
                   *, seq, tm, cw):
    t, d = x.shape
    return pl.pallas_call(
        functools.partial(_conv_kernel, tiles_per_batch=seq // tm, cw=cw),
        grid=(t // tm,),
        in_specs=[
            pl.BlockSpec((tm, d), lambda i: (i, 0)),
            *_mod_specs(layer, 0, 1, d),
            _resident((1, d)),
            _resident((d, 2 * d), j),
            _resident((1, 2 * d)),
            _resident((CONV_WIDTH, d)),
            _resident((1, d)),
            _resident((1, d)),
            _resident((1, d)),
            _resident((d, d), j),
            _resident((1, d)),
        ],
        out_specs=pl.BlockSpec((tm, d), lambda i: (i, 0)),
        out_shape=jax.ShapeDtypeStruct((t, d), F32),
        scratch_shapes=[pltpu.VMEM((CONV_HALO, d), F32)],
        compiler_params=_compiler_params(1),
        name="conv_sublayer",
    )(x, mods, mods, mods, g, w_pw1, b_pw1, w_dw, b_dw, ln_g, ln_b, w_pw2, b_pw2)


def _pool_kernel(x_ref, sh_ref, sc_ref, gt_ref, g_ref, pw_ref, pb_ref, ls_ref, o_ref, hx_ref,
                 *, tiles_per_batch):
    i = pl.program_id(0)
    b = i // tiles_per_batch
    tile_in_seq = i % tiles_per_batch
    tm, d = x_ref.shape
    gd = d // len(POOL_WINDOWS)

    @pl.when(tile_in_seq == 0)
    def _():
        hx_ref[0:POOL_HALO, :] = jnp.zeros((POOL_HALO, d), F32)

    x = x_ref[...]
    hx_ref[POOL_HALO:, :] = _rms_modulate(x, g_ref[...], sh_ref[pl.ds(b, 1), :], sc_ref[pl.ds(b, 1), :])
    pos = tile_in_seq * tm + lax.broadcasted_iota(jnp.int32, (tm, 1), 0)
    gate = gt_ref[pl.ds(b, 1), :]

    for gi, win in enumerate(POOL_WINDOWS):
        cs = slice(gi * gd, (gi + 1) * gd)
        h = hx_ref[POOL_HALO:, cs]
        tot = h
        for back in range(1, win):
            tot = tot + hx_ref[POOL_HALO - back:POOL_HALO - back + tm, cs]
        count = jnp.minimum(pos + 1, win).astype(F32)
        p = (tot / count - h).astype(BF16)
        y = (_dot(p, pw_ref[gi]) + pb_ref[:, cs]) * ls_ref[:, cs]
        o_ref[:, cs] = x[:, cs] + gate[:, cs] * y

    hx_ref[0:POOL_HALO, :] = hx_ref[tm:tm + POOL_HALO, :]


def _pool_sublayer(x, mods, layer, g, pool_w, pool_b, pool_scale, *, seq, tm):
    t, d = x.shape
    groups, gd, _ = pool_w.shape
    return pl.pallas_call(
        functools.partial(_pool_kernel, tiles_per_batch=seq // tm),
        grid=(t // tm,),
        in_specs=[
            pl.BlockSpec((tm, d), lambda i: (i, 0)),
            *_mod_specs(layer, 0, 1, d),
            pl.BlockSpec((1, d), lambda i: (0, 0)),
            pl.BlockSpec((groups, gd, gd), lambda i: (0, 0, 0)),
            pl.BlockSpec((1, d), lambda i: (0, 0)),
            pl.BlockSpec((1, d), lambda i: (0, 0)),
        ],
        out_specs=pl.BlockSpec((tm, d), lambda i: (i, 0)),
        out_shape=jax.ShapeDtypeStruct((t, d), F32),
        scratch_shapes=[pltpu.VMEM((tm + POOL_HALO, d), F32)],
        compiler_params=_compiler_params(1),
        name="pool_sublayer",
    )(x, mods, mods, mods, g, pool_w, pool_b, pool_scale)


def kernel(x, c, w_ada, b_ada, norm_mix_g, norm_ffn_g,
           sgu_w_in, sgu_b_in, sgu_v_g, sgu_v_b, sgu_w_s, sgu_b_s, sgu_w_out, sgu_b_out,
           conv_w_pw1, conv_b_pw1, conv_w_dw, conv_b_dw, conv_ln_g, conv_ln_b, conv_w_pw2, conv_b_pw2,
           pool_w, pool_b, pool_scale, mlp_w1, mlp_w2, final_g):
    batch, seq, d = x.shape
    depth = w_ada.shape[0]
    row = lambda v: v.reshape(1, -1)

    mods = _modulation(c, w_ada, b_ada)
    xt = x.reshape(batch * seq, d)
    sgu_w_in, sgu_w_out = sgu_w_in.astype(BF16), sgu_w_out.astype(BF16)
    conv_w_pw1, conv_w_pw2 = conv_w_pw1.astype(BF16), conv_w_pw2.astype(BF16)
    mlp_w1, mlp_w2 = mlp_w1.astype(BF16), mlp_w2.astype(BF16)

    for layer in range(depth):
        kind, j = layer % N_MIXERS, layer // N_MIXERS
        g_mix = row(norm_mix_g[layer])
        if kind == 0:
            xt = _sgu_sublayer(
                xt, mods, layer, j, g_mix, sgu_w_in, row(sgu_b_in[j]), row(sgu_v_g[j]), row(sgu_v_b[j]),
                sgu_w_s[j], sgu_b_s[j], sgu_w_out, row(sgu_b_out[j]), seq=seq, tm=MIXER_TOKEN_TILE,
                cw=SGU_COLUMN_GROUP)
        elif kind == 1:
            xt = _conv_sublayer(
                xt, mods, layer, j, g_mix, conv_w_pw1, row(conv_b_pw1[j]), conv_w_dw[j], row(conv_b_dw[j]),
                row(conv_ln_g[j]), row(conv_ln_b[j]), conv_w_pw2, row(conv_b_pw2[j]), seq=seq,
                tm=MIXER_TOKEN_TILE, cw=CONV_COLUMN_GROUP)
        else:
            xt = _pool_sublayer(
                xt, mods, layer, g_mix, pool_w[j].astype(BF16), row(pool_b[j]), row(pool_scale[j]),
                seq=seq, tm=MIXER_TOKEN_TILE)
        xt = _mlp_sublayer(
            xt, mods, layer, row(norm_ffn_g[layer]), mlp_w1, mlp_w2,
            row(final_g) if layer == depth - 1 else None, seq=seq, **MLP_TILES)

    return xt.reshape(batch, seq, d)
```
